```python
import math
import jax
import jax.numpy as jnp
from jax import lax
import numpy as np

D_MODEL = 1024
BATCH = 2
SEQ = 8192
DEPTH = 1
DEC_BATCH = 128
DEC_SEQ = 1
PAST_LEN = 8192
PAGE_SIZE = 128

HEAD_DIM = 64
NSA_HEADS = 8
CMP_BLOCK = 32
SEL_BLOCK = 64
TOP_N = 16
WINDOW = 512
DIFF_HEADS = 4
DIFF_KV_HEADS = 2
DIFF_GROUP = DIFF_HEADS // DIFF_KV_HEADS
DIFF_VDIM = 2 * HEAD_DIM
NSA_WIDTH = NSA_HEADS * HEAD_DIM
DIFF_WIDTH = DIFF_HEADS * DIFF_VDIM
D_FF = ((8 * D_MODEL + 3 * 256 - 1) // (3 * 256)) * 256
Q_BLOCK = 128
CACHE_SIZES = (HEAD_DIM, HEAD_DIM, HEAD_DIM, HEAD_DIM, DIFF_KV_HEADS * 2 * HEAD_DIM, DIFF_KV_HEADS * DIFF_VDIM)
CACHE_W = sum(CACHE_SIZES)
WIN_W = 2 * HEAD_DIM
IN_SIZES = (NSA_WIDTH, CACHE_W, WIN_W, 3 * NSA_HEADS, DIFF_HEADS * 2 * HEAD_DIM, 2 * D_MODEL)
IN_W = sum(IN_SIZES)
FORCE_SCORE = 1e9
NEG_INF = -1e30
EPS = 1e-6

kernel_name = 'nsa_diffattn_gated_merge_step'


def _offsets(sizes):
    out, acc = [], 0
    for s in sizes[:-1]:
        acc += s
        out.append(acc)
    return out


def _w_buf():
    return min(WINDOW, PAST_LEN)


def _rms(x, g):
    xf = x.astype(jnp.float32)
    y = xf * lax.rsqrt(jnp.mean(xf * xf, axis=-1, keepdims=True) + EPS)
    return (y * g.astype(jnp.float32)).astype(x.dtype)


def _alibi(n):
    return 2.0 ** (-8.0 * jnp.arange(1, n + 1, dtype=jnp.float32) / n)


def _masked_softmax(s, mask):
    s = jnp.where(mask, s, NEG_INF)
    m = jnp.max(s, axis=-1, keepdims=True)
    e = jnp.where(mask, jnp.exp(s - m), 0.0)
    return e / jnp.maximum(jnp.sum(e, axis=-1, keepdims=True), 1e-30)


def _sdpa(q, k, v, dist, mask, slopes):
    s = jnp.einsum('...qhd,...kd->...hqk', q, k, preferred_element_type=jnp.float32) * (HEAD_DIM ** -0.5)
    s = s - slopes[:, None, None] * dist.astype(jnp.float32)[..., None, :, :]
    p = _masked_softmax(s, mask[..., None, :, :])
    o = jnp.einsum('...hqk,...kv->...qhv', p.astype(v.dtype), v)
    return o, p


def _map_query_blocks(fn, q_pos, q):
    B, T = q.shape[:2]
    qb = min(Q_BLOCK, T)
    nb = -(-T // qb)
    pad = nb * qb - T
    pos_b = jnp.pad(q_pos, (0, pad), mode='edge').reshape(nb, qb)
    q_b = jnp.pad(q, [(0, 0), (0, pad)] + [(0, 0)] * (q.ndim - 2))
    q_b = jnp.moveaxis(q_b.reshape(B, nb, qb, *q.shape[2:]), 1, 0)
    out = lax.map(lambda args: fn(*args), (pos_b, q_b))
    out = jnp.moveaxis(out, 0, 1)
    return out.reshape(B, nb * qb, *out.shape[3:])[:, :T]


def _append(past, new, total):
    parts = [new] if past is None else [past, new]
    k = sum(p.shape[1] for p in parts)
    if total > k:
        parts.append(jnp.zeros((new.shape[0], total - k, new.shape[2]), new.dtype))
    return jnp.concatenate(parts, axis=1) if len(parts) > 1 else parts[0]


def _window_banded(q, k, v, slopes):
    B, T, H, D = q.shape
    qb = min(Q_BLOCK, T)
    nb = T // qb
    span = qb + WINDOW
    kp = jnp.pad(k, ((0, 0), (WINDOW, 0), (0, 0)))
    vp = jnp.pad(v, ((0, 0), (WINDOW, 0), (0, 0)))
    idx = jnp.arange(nb)[:, None] * qb + jnp.arange(span)[None, :]
    kpos = idx - WINDOW
    qpos = jnp.arange(T).reshape(nb, qb)
    dist = qpos[:, :, None] - kpos[:, None, :]
    mask = (dist >= 0) & (dist < WINDOW) & (kpos[:, None, :] >= 0)
    o, _ = _sdpa(q.reshape(B, nb, qb, H, D), kp[:, idx], vp[:, idx], dist, mask, slopes)
    return o.reshape(B, T, H, v.shape[-1])


def _layer(x, c, past_parts, past_win, pos0, lam_init, w):
    B, T, _ = x.shape
    f32 = jnp.float32
    mod = jnp.einsum('bd,de->be', jax.nn.silu(c), w['w_ada']) + w['b_ada']
    sh1, sc1, ga1, sh2, sc2, ga2 = jnp.split(mod[:, None, :], 6, axis=-1)

    h = _rms(x, w['norm1_g']) * (1.0 + sc1) + sh1
    z = h @ w['w_in']
    q_nsa, kv_new, win_new, nsa_gl, q_dif, merge_l = jnp.split(z, _offsets(IN_SIZES), axis=-1)
    q_pos = pos0 + jnp.arange(T)
    new_parts = jnp.split(kv_new, _offsets(CACHE_SIZES), axis=-1)
    pasts = [None] * len(CACHE_SIZES) if past_parts is None else past_parts
    K = T + (0 if past_parts is None else past_parts[0].shape[1])
    k_pos = jnp.arange(K)

    slopes_a = _alibi(NSA_HEADS)
    knorm = w['nsa_k_norm']
    qn = _rms(q_nsa.reshape(B, T, NSA_HEADS, HEAD_DIM), w['nsa_q_norm'])
    Kp = -(-K // SEL_BLOCK) * SEL_BLOCK
    ck, cv, sk, sv = [_append(p, n, Kp) for p, n in zip(pasts[:4], new_parts[:4])]
    n_cmp, n_sel = Kp // CMP_BLOCK, Kp // SEL_BLOCK
    n_top = min(TOP_N, n_sel)
    kc = jnp.einsum('bnld,l->bnd', ck.reshape(B, n_cmp, CMP_BLOCK, HEAD_DIM) + w['nsa_cmp_pe'], w['nsa_cmp_wk'])
    vc = jnp.einsum('bnld,l->bnd', cv.reshape(B, n_cmp, CMP_BLOCK, HEAD_DIM), w['nsa_cmp_wv'])
    kc = _rms(kc, knorm[0])
    cmp_end = jnp.arange(n_cmp) * CMP_BLOCK + (CMP_BLOCK - 1)
    ks = _rms(sk, knorm[1]).reshape(B, n_sel, SEL_BLOCK, HEAD_DIM)
    vs = sv.reshape(B, n_sel, SEL_BLOCK, HEAD_DIM)
    blk = jnp.arange(n_sel)

    def nsa_block(qp, qb):
        nq = qp.shape[0]
        dist_c = qp[:, None] - cmp_end[None, :]
        o_c, p_c = _sdpa(qb, kc, vc, dist_c, dist_c >= 0, slopes_a)
        imp = p_c.sum(axis=1).reshape(B, nq, n_sel, SEL_BLOCK // CMP_BLOCK).sum(-1)
        forced = (blk[None, :] == (qp // SEL_BLOCK)[:, None]) | (blk[None, :] == 0)
        score = jnp.where(forced, FORCE_SCORE, imp)
        score = jnp.where(blk[None, :] * SEL_BLOCK <= qp[:, None], score, -1.0)
        _, idx = lax.top_k(score, n_top)
        kg = jax.vmap(lambda a, i: a[i])(ks, idx).reshape(B, nq, n_top * SEL_BLOCK, HEAD_DIM)
        vg = jax.vmap(lambda a, i: a[i])(vs, idx).reshape(B, nq, n_top * SEL_BLOCK, HEAD_DIM)
        pos_g = (idx[..., None] * SEL_BLOCK + jnp.arange(SEL_BLOCK)).reshape(B, nq, n_top * SEL_BLOCK)
        dist_s = (qp[None, :, None] - pos_g)[:, :, None, :]
        o_s, _ = _sdpa(qb[:, :, None], kg, vg, dist_s, dist_s >= 0, slopes_a)
        return jnp.stack([o_c, o_s[:, :, 0]], axis=-2)

    o_cs = _map_query_blocks(nsa_block, q_pos, qn)

    w_buf = _w_buf() if past_win is None else past_win.shape[1]
    base = jnp.zeros((B, w_buf, WIN_W), x.dtype) if past_win is None else past_win
    win_all = jnp.concatenate([base, win_new], axis=1)
    win_state = win_all[:, -w_buf:]
    kw, vw = jnp.split(win_all, 2, axis=-1)
    kw = _rms(kw, knorm[2])
    if past_win is None:
        o_w = _window_banded(qn, kw[:, w_buf:], vw[:, w_buf:], slopes_a)
    else:
        wpos = pos0 - w_buf + jnp.arange(w_buf + T)
        dist_w = q_pos[:, None] - wpos[None, :]
        mask_w = (dist_w >= 0) & (dist_w < WINDOW) & (wpos[None, :] >= 0)
        o_w, _ = _sdpa(qn, kw, vw, dist_w, mask_w, slopes_a)
    g = jax.nn.sigmoid(nsa_gl.reshape(B, T, NSA_HEADS, 3, 1))
    o_a = g[..., 0, :] * o_cs[..., 0, :] + g[..., 1, :] * o_cs[..., 1, :] + g[..., 2, :] * o_w
    y_a = o_a.reshape(B, T, NSA_WIDTH) @ w['w_br_nsa']

    slopes_b = _alibi(DIFF_HEADS).reshape(DIFF_KV_HEADS, DIFF_GROUP)
    dk, dv = [_append(p, n, K) for p, n in zip(pasts[4:], new_parts[4:])]
    qd = _rms(q_dif.reshape(B, T, DIFF_KV_HEADS, DIFF_GROUP, 2, HEAD_DIM), w['diff_q_norm'])
    kd = _rms(dk.reshape(B, K, DIFF_KV_HEADS, 2, HEAD_DIM), w['diff_k_norm'])
    vd = dv.reshape(B, K, DIFF_KV_HEADS, DIFF_VDIM)
    lq1, lk1, lq2, lk2 = [w['diff_lambda'][i].astype(f32) for i in range(4)]
    lam = jnp.exp(jnp.sum(lq1 * lk1)) - jnp.exp(jnp.sum(lq2 * lk2)) + lam_init

    def diff_block(qp, qb):
        dist = qp[:, None] - k_pos[None, :]
        s = jnp.einsum('bqngmd,bknmd->bngmqk', qb, kd, preferred_element_type=f32) * (HEAD_DIM ** -0.5)
        s = s - slopes_b[:, :, None, None, None] * dist.astype(f32)
        p = _masked_softmax(s, dist >= 0)
        a = p[:, :, :, 0] - lam * p[:, :, :, 1]
        return jnp.einsum('bngqk,bknv->bqngv', a.astype(vd.dtype), vd)

    o_b = _map_query_blocks(diff_block, q_pos, qd)
    o_b = _rms(o_b, w['diff_subln']) * (1.0 - lam_init)
    y_b = o_b.reshape(B, T, DIFF_WIDTH) @ w['w_br_diff']

    g_a, g_b = jnp.split(jax.nn.sigmoid(merge_l), 2, axis=-1)
    x = x + ga1 * ((g_a * y_a + g_b * y_b) @ w['w_out'])

    h2 = _rms(x, w['norm2_g']) * (1.0 + sc2) + sh2
    u_gate, u_val = jnp.split(h2 @ w['w_ffn_in'], 2, axis=-1)
    x = x + ga2 * ((jax.nn.silu(u_gate) * u_val) @ w['w_ffn_out'])
    return x, kv_new, win_state


def setup_inputs(seed: int = 0) -> dict:
    key = jax.random.key(seed)
    ks = jax.random.split(key, 26)
    f32 = jnp.float32
    n_pages = PAST_LEN // PAGE_SIZE
    n_used = DEC_BATCH * n_pages
    n_phys = n_used + n_used // 4
    w_buf = min(WINDOW, PAST_LEN)
    L = DEPTH

    def nrm(k, shape, scale):
        return scale * jax.random.normal(k, shape, f32)

    def gain(k, shape):
        return 1.0 + 0.05 * jax.random.normal(k, shape, f32)

    page_table = jax.random.permutation(ks[4], n_phys)[:n_used].reshape(DEC_BATCH, n_pages).astype(jnp.int32)
    return {
        'x_prompt': nrm(ks[0], (BATCH, SEQ, D_MODEL), 1.0),
        'x_sample': nrm(ks[1], (DEC_BATCH, DEC_SEQ, D_MODEL), 1.0),
        'cache_kv': nrm(ks[2], (L, n_phys, PAGE_SIZE, CACHE_W), 1.0),
        'state_win_kv': nrm(ks[3], (L, DEC_BATCH, w_buf, WIN_W), 1.0),
        'page_table': page_table,
        'c_prompt': nrm(ks[5], (BATCH, D_MODEL), 1.0),
        'c_sample': nrm(ks[6], (DEC_BATCH, D_MODEL), 1.0),
        'w_ada': nrm(ks[7], (L, D_MODEL, 6 * D_MODEL), 0.5 * D_MODEL ** -0.5),
        'b_ada': nrm(ks[8], (L, 6 * D_MODEL), 0.05),
        'norm1_g': gain(ks[9], (L, D_MODEL)),
        'norm2_g': gain(ks[10], (L, D_MODEL)),
        'w_in': nrm(ks[11], (L, D_MODEL, IN_W), D_MODEL ** -0.5),
        'nsa_q_norm': gain(ks[12], (L, HEAD_DIM)),
        'nsa_k_norm': gain(ks[13], (L, 3, HEAD_DIM)),
        'nsa_cmp_wk': (1.0 + 0.1 * jax.random.normal(ks[14], (L, CMP_BLOCK), f32)) / CMP_BLOCK,
        'nsa_cmp_wv': (1.0 + 0.1 * jax.random.normal(ks[15], (L, CMP_BLOCK), f32)) / CMP_BLOCK,
        'nsa_cmp_pe': nrm(ks[16], (L, CMP_BLOCK, HEAD_DIM), 0.1),
        'diff_q_norm': gain(ks[17], (L, HEAD_DIM)),
        'diff_k_norm': gain(ks[18], (L, HEAD_DIM)),
        'diff_lambda': nrm(ks[19], (L, 4, HEAD_DIM), 0.1),
        'diff_subln': gain(ks[20], (L, DIFF_VDIM)),
        'w_br_nsa': nrm(ks[21], (L, NSA_WIDTH, D_MODEL), NSA_WIDTH ** -0.5),
        'w_br_diff': nrm(ks[22], (L, DIFF_WIDTH, D_MODEL), DIFF_WIDTH ** -0.5),
        'w_out': nrm(ks[23], (L, D_MODEL, D_MODEL), D_MODEL ** -0.5),
        'w_ffn_in': nrm(ks[24], (L, D_MODEL, 2 * D_FF), D_MODEL ** -0.5),
        'w_ffn_out': nrm(ks[25], (L, D_FF, D_MODEL), D_FF ** -0.5),
    }


def reference(x_prompt, x_sample, cache_kv, state_win_kv, page_table, c_prompt, c_sample,
              w_ada, b_ada, norm1_g, norm2_g, w_in, nsa_q_norm, nsa_k_norm, nsa_cmp_wk, nsa_cmp_wv,
              nsa_cmp_pe, diff_q_norm, diff_k_norm, diff_lambda, diff_subln, w_br_nsa, w_br_diff,
              w_out, w_ffn_in, w_ffn_out):
    n_pages = page_table.shape[1]
    past_len = n_pages * PAGE_SIZE
    dec_b = x_sample.shape[0]
    bounds = [0] + _offsets(CACHE_SIZES) + [CACHE_W]
    xp, xs = x_prompt, x_sample
    kv_p, kv_s, win_p, win_s = [], [], [], []
    for l in range(DEPTH):
        w = {
            'w_ada': w_ada[l], 'b_ada': b_ada[l], 'norm1_g': norm1_g[l], 'norm2_g': norm2_g[l],
            'w_in': w_in[l], 'nsa_q_norm': nsa_q_norm[l], 'nsa_k_norm': nsa_k_norm[l],
            'nsa_cmp_wk': nsa_cmp_wk[l], 'nsa_cmp_wv': nsa_cmp_wv[l], 'nsa_cmp_pe': nsa_cmp_pe[l],
            'diff_q_norm': diff_q_norm[l], 'diff_k_norm': diff_k_norm[l], 'diff_lambda': diff_lambda[l],
            'diff_subln': diff_subln[l], 'w_br_nsa': w_br_nsa[l], 'w_br_diff': w_br_diff[l],
            'w_out': w_out[l], 'w_ffn_in': w_ffn_in[l], 'w_ffn_out': w_ffn_out[l],
        }
        lam_init = 0.8 - 0.6 * math.exp(-0.3 * l)
        past_parts = [cache_kv[l, page_table, :, a:b].reshape(dec_b, past_len, b - a)
                      for a, b in zip(bounds[:-1], bounds[1:])]
        xp, kvp, wp = _layer(xp, c_prompt, None, None, 0, lam_init, w)
        xs, kvs, ws = _layer(xs, c_sample, past_parts, state_win_kv[l], past_len, lam_init, w)
        kv_p.append(kvp)
        kv_s.append(kvs)
        win_p.append(wp)
        win_s.append(ws)
    return (xp, xs, jnp.stack(kv_p), jnp.stack(kv_s), jnp.stack(win_p), jnp.stack(win_s))
```

```python
import functools
import math

import jax
import jax.numpy as jnp
from jax import lax
from jax.experimental import pallas as pl
from jax.experimental.pallas import tpu as pltpu

F32 = jnp.float32
BF16 = jnp.bfloat16

HEAD_DIM = 64
NSA_HEADS = 8
CMP_BLOCK = 32
SEL_BLOCK = 64
TOP_N = 16
WINDOW = 512
DIFF_KV_HEADS = 2
DIFF_GROUP = 2
DIFF_VDIM = 2 * HEAD_DIM
PAGE_SIZE = 128
EPS = 1e-6
NEG_INF = -1e30
FORCE_SCORE = 1e9
QK_SCALE = HEAD_DIM ** -0.5

NSA_W = NSA_HEADS * HEAD_DIM
KV_W = 4 * HEAD_DIM + 2 * DIFF_KV_HEADS * 2 * HEAD_DIM
WIN_W = 2 * HEAD_DIM
GL_W = 3 * NSA_HEADS
QD_W = DIFF_KV_HEADS * DIFF_GROUP * 2 * HEAD_DIM
DK_W = DIFF_KV_HEADS * 2 * HEAD_DIM
DV_W = DIFF_KV_HEADS * DIFF_VDIM
LANES = 128

VMEM_LIMIT_BYTES = 56 * 1024 * 1024


def _cparams(n_grid_dims):
    return pltpu.CompilerParams(
        dimension_semantics=("arbitrary",) * n_grid_dims,
        vmem_limit_bytes=VMEM_LIMIT_BYTES,
    )


def _dot(a, b):
    return jnp.dot(a, b, preferred_element_type=F32)


def _dot_nt(a, b):
    return lax.dot_general(a, b, (((1,), (1,)), ((), ())), preferred_element_type=F32)


def _group_sumsq(x, gmat):
    x2 = x * x
    hi = x2.astype(BF16)
    lo = (x2 - hi.astype(F32)).astype(BF16)
    return _dot(hi, gmat) + _dot(lo, gmat)


def _lane_iota(shape):
    return lax.broadcasted_iota(jnp.int32, shape, len(shape) - 1)


def _alibi_slopes(n):
    return [2.0 ** (-8.0 * (i + 1) / n) for i in range(n)]


def _row_const(rows_per_group, values):
    n = len(values)
    row = lax.broadcasted_iota(jnp.int32, (n * rows_per_group, 1), 0)
    out = jnp.full((n * rows_per_group, 1), values[0], F32)
    for g in range(1, n):
        out = jnp.where(row >= g * rows_per_group, values[g], out)
    return out


def _online_step(s, v, m, l, acc):
    m_new = jnp.maximum(m, jnp.max(s, axis=-1, keepdims=True))
    alpha = jnp.exp(m - m_new)
    p = jnp.exp(s - m_new)
    l = alpha * l + jnp.sum(p, axis=-1, keepdims=True)
    acc = alpha * acc + _dot(p.astype(BF16), v)
    return m_new, l, acc


def _finish(m, l, acc):
    return jnp.where(m > 0.5 * NEG_INF, acc / jnp.maximum(l, 1e-30), 0.0)


def _ada_kernel(c_ref, w_ref, b_ref, o_ref):
    c = c_ref[...]
    sc = c * jax.nn.sigmoid(c)
    o_ref[...] = _dot(sc.astype(BF16), w_ref[...].astype(BF16)) + b_ref[...]


def _ada(c, w_ada, b_ada):
    m, d = c.shape
    n = w_ada.shape[1]
    tn = d
    return pl.pallas_call(
        _ada_kernel,
        grid=(n // tn,),
        in_specs=[
            pl.BlockSpec((m, d), lambda j: (0, 0)),
            pl.BlockSpec((d, tn), lambda j: (0, j)),
            pl.BlockSpec((1, tn), lambda j: (0, j)),
        ],
        out_specs=pl.BlockSpec((m, tn), lambda j: (0, j)),
        out_shape=jax.ShapeDtypeStruct((m, n), F32),
        compiler_params=_cparams(1),
        name="ada",
    )(c, w_ada, b_ada.reshape(1, n))


_C_QN = 0
_C_KV = _C_QN + NSA_W
_C_WIN = _C_KV + KV_W
_C_QD = _C_WIN + WIN_W
_C_MG = _C_QD + QD_W


def _inproj_kernel(*refs, compress):
    (x_ref, sh_ref, sc_ref, g1_ref, w_ref, gmat_ref, qg_ref, dqg_ref, selg_ref, wing_ref,
     dkg_ref) = refs[:11]
    n_in = 11
    if compress:
        wk_ref, wv_ref, pe_ref, kcg_ref = refs[11:15]
        n_in = 15
    (kv_ref, win_ref, qn_ref, qd_ref, selkv_ref, winkv_ref, kd_ref, vd_ref, gl_ref,
     gm_ref) = refs[n_in:n_in + 10]
    d_model = x_ref.shape[-1]
    c_gl = _C_MG + 2 * d_model

    x = x_ref[...]
    ms = jnp.mean(x * x, axis=-1, keepdims=True)
    h = x * lax.rsqrt(ms + EPS) * g1_ref[...]
    h = h * (1.0 + sc_ref[0]) + sh_ref[0]
    hb = h.astype(BF16)

    def proj(a, b):
        return _dot(hb, w_ref[:, a:b])

    def normed(z, width, gain):
        ss = _group_sumsq(z, gmat_ref[0:width, 0:width])
        return z * lax.rsqrt(ss * (1.0 / HEAD_DIM) + EPS) * gain

    def half_normed(z, gain):
        ss = _group_sumsq(z, gmat_ref[0:LANES, 0:LANES])
        r = jnp.where(_lane_iota(z.shape) < HEAD_DIM, lax.rsqrt(ss * (1.0 / HEAD_DIM) + EPS), 1.0)
        return z * r * gain

    qn_ref[...] = normed(proj(_C_QN, _C_KV), NSA_W, qg_ref[...]).astype(BF16)
    qd_ref[...] = normed(proj(_C_QD, _C_MG), QD_W, dqg_ref[...]).astype(BF16)

    kv = proj(_C_KV, _C_WIN)
    kv_ref[...] = kv
    selkv_ref[...] = half_normed(kv[:, 2 * HEAD_DIM:4 * HEAD_DIM], selg_ref[...]).astype(BF16)
    dk = kv[:, 4 * HEAD_DIM:4 * HEAD_DIM + DK_W]
    kd_ref[...] = normed(dk, DK_W, dkg_ref[...]).astype(BF16)
    vd_ref[...] = kv[:, 4 * HEAD_DIM + DK_W:].astype(BF16)

    win = proj(_C_WIN, _C_QD)
    win_ref[...] = win
    winkv_ref[...] = half_normed(win, wing_ref[...]).astype(BF16)

    gm_ref[...] = jax.nn.sigmoid(proj(_C_MG, c_gl))
    gl_ref[...] = jax.nn.sigmoid(proj(c_gl, c_gl + LANES))

    if compress:
        kcvc_ref = refs[n_in + 10]
        cmp = kv[:, 0:2 * HEAD_DIM]
        kc = jnp.dot(wk_ref[...], cmp + pe_ref[...], precision=lax.Precision.HIGHEST,
                     preferred_element_type=F32)
        vc = jnp.dot(wv_ref[...], cmp, precision=lax.Precision.HIGHEST,
                     preferred_element_type=F32)
        is_k = _lane_iota(kc.shape) < HEAD_DIM
        ss = jnp.sum(jnp.where(is_k, kc * kc, 0.0), axis=-1, keepdims=True)
        kc = kc * lax.rsqrt(ss * (1.0 / HEAD_DIM) + EPS) * kcg_ref[...]
        kcvc_ref[...] = jnp.where(is_k, kc, vc)


def _inproj(x2d, sh, sc, tm, p, compress):
    n, d = x2d.shape
    steps = n // tm
    r = sh.shape[1]
    steps_per_group = steps // sh.shape[0]
    w = p["w_in"]
    wcols = w.shape[1]

    def const(a):
        return pl.BlockSpec(a.shape, lambda i: (0,) * a.ndim)

    def rows(width):
        return pl.BlockSpec((tm, width), lambda i: (i, 0))

    mod_spec = pl.BlockSpec((1, r, d), lambda i: (i // steps_per_group, 0, 0))
    ins = [x2d, sh, sc, p["g1"], w, p["gmat"], p["qg"], p["dqg"], p["selg"], p["wing"], p["dkg"]]
    in_specs = [rows(d), mod_spec, mod_spec, const(p["g1"]), const(w), const(p["gmat"]),
                const(p["qg"]), const(p["dqg"]), const(p["selg"]), const(p["wing"]),
                const(p["dkg"])]
    outs = [(KV_W, F32), (WIN_W, F32), (NSA_W, BF16), (QD_W, BF16), (LANES, BF16), (LANES, BF16),
            (DK_W, BF16), (DV_W, BF16), (LANES, F32), (2 * d, F32)]
    out_shape = [jax.ShapeDtypeStruct((n, wd), dt) for wd, dt in outs]
    out_specs = [rows(wd) for wd, _ in outs]
    if compress:
        ins += [p["wk"], p["wv"], p["pe"], p["kcg"]]
        in_specs += [const(p["wk"]), const(p["wv"]), const(p["pe"]), const(p["kcg"])]
        out_shape.append(jax.ShapeDtypeStruct((n // CMP_BLOCK, LANES), F32))
        out_specs.append(pl.BlockSpec((tm // CMP_BLOCK, LANES), lambda i: (i, 0)))
    del wcols
    return pl.pallas_call(
        functools.partial(_inproj_kernel, compress=compress),
        grid=(steps,),
        in_specs=in_specs,
        out_specs=out_specs,
        out_shape=out_shape,
        compiler_params=_cparams(1),
        name="inproj_cmp" if compress else "inproj",
    )(*ins)


def _stack_heads(q, n_heads):
    return jnp.concatenate([q[:, h * HEAD_DIM:(h + 1) * HEAD_DIM] for h in range(n_heads)], axis=0)


def _pair_sum(x):
    n2 = x.shape[1]
    r = lax.broadcasted_iota(jnp.int32, (n2, n2 // 2), 0)
    c = lax.broadcasted_iota(jnp.int32, (n2, n2 // 2), 1)
    pmat = jnp.where((r >> 1) == c, 1.0, 0.0).astype(BF16)
    hi = x.astype(BF16)
    r1 = x - hi.astype(F32)
    mid = r1.astype(BF16)
    lo = (r1 - mid.astype(F32)).astype(BF16)
    return _dot(hi, pmat) + _dot(mid, pmat) + _dot(lo, pmat)


def _top_blocks(score, n_top):
    idx = _lane_iota(score.shape).astype(F32)
    big = float(score.shape[1])

    def body(_, carry):
        work, sel = carry
        mx = jnp.max(work, axis=-1, keepdims=True)
        first = jnp.min(jnp.where(work == mx, idx, big), axis=-1, keepdims=True)
        hit = idx == first
        return jnp.where(hit, -2.0, work), jnp.where(hit, 1.0, sel)

    _, sel = lax.fori_loop(0, n_top, body, (score, jnp.zeros_like(score)))
    return sel


def _expand_blocks(sel_bf, first_block, n_keys):
    nb = sel_bf.shape[1]
    b = lax.broadcasted_iota(jnp.int32, (nb, n_keys), 0)
    j = lax.broadcasted_iota(jnp.int32, (nb, n_keys), 1)
    emat = jnp.where(b == first_block + (j >> 6), 1.0, 0.0).astype(BF16)
    return _dot(sel_bf, emat)


def _nsa_kernel(qn_ref, kcvc_ref, selkv_ref, winkv_ref, gate_ref, o_ref, *, tq):
    qi = pl.program_id(1)
    nh = NSA_HEADS
    rows = nh * tq
    tk = tq
    q0 = qi * tq
    qs = _stack_heads(qn_ref[...], nh)
    row = lax.broadcasted_iota(jnp.int32, (rows, 1), 0)
    qpos = q0 + (row & (tq - 1))
    slope = _row_const(tq, _alibi_slopes(nh))
    qpos_t = q0 + lax.broadcasted_iota(jnp.int32, (tq, 1), 0)

    kcvc = kcvc_ref[...]
    n_cmp = kcvc.shape[0]
    kc = kcvc[:, :HEAD_DIM].astype(BF16)
    vc = kcvc[:, HEAD_DIM:].astype(BF16)
    cmp_end = _lane_iota((1, n_cmp)) * CMP_BLOCK + (CMP_BLOCK - 1)
    dist_c = qpos - cmp_end
    ok_c = dist_c >= 0
    s = _dot_nt(qs, kc) - slope * dist_c.astype(F32)
    s = jnp.where(ok_c, s, NEG_INF)
    e = jnp.where(ok_c, jnp.exp(s - jnp.max(s, axis=-1, keepdims=True)), 0.0)
    p_c = e / jnp.maximum(jnp.sum(e, axis=-1, keepdims=True), 1e-30)
    o_c = _dot(p_c.astype(BF16), vc)
    p_heads = p_c[0:tq]
    for h in range(1, nh):
        p_heads = p_heads + p_c[h * tq:(h + 1) * tq]
    imp = _pair_sum(p_heads)
    n_sel = imp.shape[1]
    blk = _lane_iota((1, n_sel))
    forced = (blk == (qpos_t >> 6)) | (blk == 0)
    score = jnp.where(forced, FORCE_SCORE, imp)
    score = jnp.where(blk * SEL_BLOCK <= qpos_t, score, -1.0)
    sel = _top_blocks(score, min(TOP_N, n_sel)).astype(BF16)

    def init():
        return (jnp.full((rows, 1), NEG_INF, F32), jnp.zeros((rows, 1), F32),
                jnp.zeros((rows, HEAD_DIM), F32))

    def key_tile(ref, kt):
        t = ref[pl.ds(pl.multiple_of(kt * tk, tk), tk), :]
        kpos = kt * tk + _lane_iota((1, tk))
        return t[:, :HEAD_DIM], t[:, HEAD_DIM:], kpos

    def sel_step(kt, carry):
        k, v, kpos = key_tile(selkv_ref, kt)
        em = _expand_blocks(sel, kt * (tk // SEL_BLOCK), tk)
        ok = (em > 0.5) & (kpos <= qpos_t)
        ok = jnp.concatenate([ok.astype(F32)] * nh, axis=0) > 0.5
        sc = _dot_nt(qs, k) - slope * (qpos - kpos).astype(F32)
        return _online_step(jnp.where(ok, sc, NEG_INF), v, *carry)

    o_s = _finish(*lax.fori_loop(0, qi + 1, sel_step, init()))

    def win_step(kt, carry):
        k, v, kpos = key_tile(winkv_ref, kt)
        dist = qpos - kpos
        ok = (dist >= 0) & (dist < WINDOW)
        sc = _dot_nt(qs, k) - slope * dist.astype(F32)
        return _online_step(jnp.where(ok, sc, NEG_INF), v, *carry)

    first_tile = jnp.maximum(qi - WINDOW // tk, 0)
    o_w = _finish(*lax.fori_loop(first_tile, qi + 1, win_step, init()))

    gate = gate_ref[...]
    for h in range(nh):
        r0, r1 = h * tq, (h + 1) * tq
        o = (gate[:, 3 * h:3 * h + 1] * o_c[r0:r1] + gate[:, 3 * h + 1:3 * h + 2] * o_s[r0:r1]
             + gate[:, 3 * h + 2:3 * h + 3] * o_w[r0:r1])
        o_ref[:, h * HEAD_DIM:(h + 1) * HEAD_DIM] = o.astype(o_ref.dtype)


def _nsa_prompt(qn, kcvc, selkv, winkv, gates, batch, seq, tq):
    nq = seq // tq
    n_cmp = seq // CMP_BLOCK

    def qrow(width):
        return pl.BlockSpec((tq, width), lambda b, i: (b * nq + i, 0))

    def per_batch(rows, width):
        return pl.BlockSpec((rows, width), lambda b, i: (b, 0))

    return pl.pallas_call(
        functools.partial(_nsa_kernel, tq=tq),
        grid=(batch, nq),
        in_specs=[qrow(NSA_W), per_batch(n_cmp, LANES), per_batch(seq, LANES),
                  per_batch(seq, LANES), qrow(LANES)],
        out_specs=qrow(NSA_W),
        out_shape=jax.ShapeDtypeStruct((batch * seq, NSA_W), BF16),
        compiler_params=_cparams(2),
        name="nsa_prompt",
    )(qn, kcvc, selkv, winkv, gates)


def _lambda_full(lam_ref, lam_init):
    dl = lam_ref[...]
    a = jnp.sum(dl[0:1] * dl[1:2], axis=-1, keepdims=True)
    b = jnp.sum(dl[2:3] * dl[3:4], axis=-1, keepdims=True)
    return jnp.exp(a) - jnp.exp(b) + lam_init


def _subln(o, gain, lam_init):
    ms = jnp.mean(o * o, axis=-1, keepdims=True)
    return o * lax.rsqrt(ms + EPS) * gain * (1.0 - lam_init)


def _diff_kernel(qd_ref, kd_ref, vd_ref, lam_ref, subln_ref, o_ref, *, tq, lam_init):
    n = pl.program_id(1)
    qi = pl.program_id(2)
    tk = tq
    g = DIFF_GROUP
    rows = 2 * g * tq
    q = qd_ref[...]
    qm = [jnp.concatenate([q[:, (gg * 2 + m) * HEAD_DIM:(gg * 2 + m + 1) * HEAD_DIM]
                           for gg in range(g)], axis=0) for m in range(2)]
    row = lax.broadcasted_iota(jnp.int32, (rows, 1), 0)
    qpos = qi * tq + (row & (tq - 1))
    slopes = _alibi_slopes(DIFF_KV_HEADS * g)
    sl = [_row_const(tq, [slopes[kv * g + gg] for gg in range(g)] * 2) for kv in range(DIFF_KV_HEADS)]
    slope = jnp.where(n == 0, sl[0], sl[1])

    def scores(kt):
        t = kd_ref[pl.ds(pl.multiple_of(kt * tk, tk), tk), :]
        kpos = kt * tk + _lane_iota((1, tk))
        dist = qpos - kpos
        sc = jnp.concatenate([_dot_nt(qm[0], t[:, :HEAD_DIM]), _dot_nt(qm[1], t[:, HEAD_DIM:])],
                             axis=0)
        return sc - slope * dist.astype(F32), dist

    def values(kt):
        return vd_ref[pl.ds(pl.multiple_of(kt * tk, tk), tk), :]

    def full_step(kt, carry):
        sc, _ = scores(kt)
        return _online_step(sc, values(kt), *carry)

    carry = (jnp.full((rows, 1), NEG_INF, F32), jnp.zeros((rows, 1), F32),
             jnp.zeros((rows, DIFF_VDIM), F32))
    carry = lax.fori_loop(0, qi, full_step, carry)
    sc, dist = scores(qi)
    o = _finish(*_online_step(jnp.where(dist >= 0, sc, NEG_INF), values(qi), *carry))
    lam = _lambda_full(lam_ref, lam_init)
    for gg in range(g):
        r0 = gg * tq
        d = o[r0:r0 + tq] - lam * o[g * tq + r0:g * tq + r0 + tq]
        o_ref[:, gg * DIFF_VDIM:(gg + 1) * DIFF_VDIM] = _subln(d, subln_ref[...], lam_init).astype(o_ref.dtype)


def _diff_prompt(qd, kd, vd, lam, subln, batch, seq, tq, lam_init):
    nq = seq // tq
    per_head_q = DIFF_GROUP * 2 * HEAD_DIM
    return pl.pallas_call(
        functools.partial(_diff_kernel, tq=tq, lam_init=lam_init),
        grid=(batch, DIFF_KV_HEADS, nq),
        in_specs=[
            pl.BlockSpec((tq, per_head_q), lambda b, n, i: (b * nq + i, n)),
            pl.BlockSpec((seq, 2 * HEAD_DIM), lambda b, n, i: (b, n)),
            pl.BlockSpec((seq, DIFF_VDIM), lambda b, n, i: (b, n)),
            pl.BlockSpec(lam.shape, lambda b, n, i: (0, 0)),
            pl.BlockSpec(subln.shape, lambda b, n, i: (0, 0)),
        ],
        out_specs=pl.BlockSpec((tq, DIFF_GROUP * DIFF_VDIM), lambda b, n, i: (b * nq + i, n)),
        out_shape=jax.ShapeDtypeStruct((batch * seq, QD_W), BF16),
        compiler_params=_cparams(3),
        name="diff_prompt",
    )(qd, kd, vd, lam, subln)


_QROWS = 16
_PAGES_PER_STEP = 8


def _decode_kernel(pt_ref, *refs, n_chunks, lam_init):
    del pt_ref
    npg = _PAGES_PER_STEP
    pages = refs[:npg]
    (qn_ref, qd_ref, selkv_new_ref, winkv_new_ref, kd_new_ref, vd_new_ref, gate_ref, win_ref,
     gmat_ref, wk_ref, wv_ref, pe_ref, kcg_ref, selg_ref, wing_ref, dkg_ref, lam_ref,
     subln_ref) = refs[npg:npg + 18]
    oa_ref, ob_ref = refs[npg + 18:npg + 20]
    (cmp_buf, kd_buf, vd_buf, stash, kcvc_s, dm_s, dl_s, dacc_s) = refs[npg + 20:]
    c = pl.program_id(1)
    rows_per_step = npg * PAGE_SIZE
    past_len = n_chunks * rows_per_step
    nh = NSA_HEADS

    for i in range(npg):
        pg = pages[i]
        r0 = i * PAGE_SIZE
        cmp_buf[r0:r0 + PAGE_SIZE, :] = pg[:, 0:2 * HEAD_DIM]
        sk = pg[:, 2 * HEAD_DIM:4 * HEAD_DIM]
        ss = _group_sumsq(sk, gmat_ref[0:LANES, 0:LANES])
        r = jnp.where(_lane_iota(sk.shape) < HEAD_DIM, lax.rsqrt(ss * (1.0 / HEAD_DIM) + EPS), 1.0)
        base = pl.multiple_of(c * rows_per_step + r0, PAGE_SIZE)
        stash[pl.ds(base, PAGE_SIZE), :] = (sk * r * selg_ref[...]).astype(BF16)
        dk = pg[:, 4 * HEAD_DIM:4 * HEAD_DIM + DK_W]
        ss = _group_sumsq(dk, gmat_ref[0:DK_W, 0:DK_W])
        kd_buf[r0:r0 + PAGE_SIZE, :] = (dk * lax.rsqrt(ss * (1.0 / HEAD_DIM) + EPS)
                                        * dkg_ref[...]).astype(BF16)
        vd_buf[r0:r0 + PAGE_SIZE, :] = pg[:, 4 * HEAD_DIM + DK_W:].astype(BF16)

    cmp = cmp_buf[...]
    kc = jnp.dot(wk_ref[...], cmp + pe_ref[...], precision=lax.Precision.HIGHEST,
                 preferred_element_type=F32)
    vc = jnp.dot(wv_ref[...], cmp, precision=lax.Precision.HIGHEST, preferred_element_type=F32)
    is_k = _lane_iota(kc.shape) < HEAD_DIM
    ss = jnp.sum(jnp.where(is_k, kc * kc, 0.0), axis=-1, keepdims=True)
    kc = kc * lax.rsqrt(ss * (1.0 / HEAD_DIM) + EPS) * kcg_ref[...]
    cmp_rows = rows_per_step // CMP_BLOCK
    kcvc_s[pl.ds(pl.multiple_of(c * cmp_rows, cmp_rows), cmp_rows), :] = jnp.where(is_k, kc, vc)

    slopes = _alibi_slopes(DIFF_KV_HEADS * DIFF_GROUP)
    dslope = _row_const(1, [slopes[r // 2] for r in range(8)] + [0.0] * (_QROWS - 8))
    qd = qd_ref[...]

    @pl.when(c == 0)
    def _():
        dm_s[...] = jnp.full(dm_s.shape, NEG_INF, F32)
        dl_s[...] = jnp.zeros(dl_s.shape, F32)
        dacc_s[...] = jnp.zeros(dacc_s.shape, F32)

    kpos = c * rows_per_step + _lane_iota((1, rows_per_step))
    sc = _dot_nt(qd, kd_buf[...]) - dslope * (past_len - kpos).astype(F32)
    dm, dl, dacc = _online_step(sc, vd_buf[...], dm_s[...], dl_s[...], dacc_s[...])
    dm_s[...] = dm
    dl_s[...] = dl
    dacc_s[...] = dacc

    @pl.when(c == n_chunks - 1)
    def _():
        kd_new = kd_new_ref[...].astype(F32)
        s_new = jnp.sum(qd.astype(F32) * kd_new, axis=-1, keepdims=True)
        m2 = jnp.maximum(dm, s_new)
        alpha = jnp.exp(dm - m2)
        p_new = jnp.exp(s_new - m2)
        l2 = alpha * dl + p_new
        acc2 = alpha * dacc + p_new * vd_new_ref[...].astype(F32)
        od = acc2 / l2
        lam = _lambda_full(lam_ref, lam_init)
        for n in range(DIFF_KV_HEADS):
            for g in range(DIFF_GROUP):
                r = (n * DIFF_GROUP + g) * 2
                lo, hi = n * DIFF_VDIM, (n + 1) * DIFF_VDIM
                d = od[r:r + 1, lo:hi] - lam * od[r + 1:r + 2, lo:hi]
                col = (n * DIFF_GROUP + g) * DIFF_VDIM
                ob_ref[:, col:col + DIFF_VDIM] = _subln(d, subln_ref[...], lam_init).astype(ob_ref.dtype)

        qn = qn_ref[...]
        aslope = _row_const(1, _alibi_slopes(nh) + [0.0] * (_QROWS - nh))
        kcvc = kcvc_s[...]
        n_cmp = kcvc.shape[0]
        cmp_end = _lane_iota((1, n_cmp)) * CMP_BLOCK + (CMP_BLOCK - 1)
        s = _dot_nt(qn, kcvc[:, :HEAD_DIM].astype(BF16)) - aslope * (past_len - cmp_end).astype(F32)
        e = jnp.exp(s - jnp.max(s, axis=-1, keepdims=True))
        p_c = e / jnp.sum(e, axis=-1, keepdims=True)
        o_c = _dot(p_c.astype(BF16), kcvc[:, HEAD_DIM:].astype(BF16))
        head_row = lax.broadcasted_iota(jnp.int32, (_QROWS, 1), 0) < nh
        p_sum = jnp.sum(jnp.where(head_row, p_c, 0.0), axis=0, keepdims=True)
        imp = _pair_sum(jnp.broadcast_to(p_sum, (8, n_cmp)))
        n_blk = imp.shape[1]
        score = jnp.where(_lane_iota(imp.shape) == 0, FORCE_SCORE, imp)
        sel = _top_blocks(score, min(TOP_N - 1, n_blk)).astype(BF16)

        def sel_step(t, carry):
            base = pl.multiple_of(t * rows_per_step, rows_per_step)
            kv = stash[pl.ds(base, rows_per_step), :]
            em = _expand_blocks(sel, t * (rows_per_step // SEL_BLOCK), rows_per_step)
            ok = jnp.concatenate([em, em], axis=0) > 0.5
            kp = t * rows_per_step + _lane_iota((1, rows_per_step))
            sc = _dot_nt(qn, kv[:, :HEAD_DIM]) - aslope * (past_len - kp).astype(F32)
            return _online_step(jnp.where(ok, sc, NEG_INF), kv[:, HEAD_DIM:], *carry)

        init = (jnp.full((_QROWS, 1), NEG_INF, F32), jnp.zeros((_QROWS, 1), F32),
                jnp.zeros((_QROWS, HEAD_DIM), F32))
        sm, sl_, sacc = lax.fori_loop(0, n_chunks, sel_step, init)
        qf = qn.astype(F32)

        def add_new(m, l, acc, kv_new):
            s_new = jnp.sum(qf * kv_new[:, :HEAD_DIM], axis=-1, keepdims=True)
            m2 = jnp.maximum(m, s_new)
            alpha = jnp.exp(m - m2)
            p_new = jnp.exp(s_new - m2)
            return (alpha * acc + p_new * kv_new[:, HEAD_DIM:]) / (alpha * l + p_new)

        o_s = add_new(sm, sl_, sacc, selkv_new_ref[...].astype(F32))

        win = win_ref[...]
        w_buf = win.shape[0]
        ss = _group_sumsq(win, gmat_ref[0:LANES, 0:LANES])
        r = jnp.where(_lane_iota(win.shape) < HEAD_DIM, lax.rsqrt(ss * (1.0 / HEAD_DIM) + EPS), 1.0)
        wkv = (win * r * wing_ref[...]).astype(BF16)
        dist_w = w_buf - _lane_iota((1, w_buf))
        sc = _dot_nt(qn, wkv[:, :HEAD_DIM]) - aslope * dist_w.astype(F32)
        sc = jnp.where(dist_w < WINDOW, sc, NEG_INF)
        o_w = add_new(*_online_step(sc, wkv[:, HEAD_DIM:], *init), winkv_new_ref[...].astype(F32))

        gate = gate_ref[...]
        for h in range(nh):
            o = (gate[:, 3 * h:3 * h + 1] * o_c[h:h + 1] + gate[:, 3 * h + 1:3 * h + 2] * o_s[h:h + 1]
                 + gate[:, 3 * h + 2:3 * h + 3] * o_w[h:h + 1])
            oa_ref[:, h * HEAD_DIM:(h + 1) * HEAD_DIM] = o.astype(oa_ref.dtype)


def _decode(l, page_table, cache, qn16, qd16, selkv_new, winkv_new, kd_new, vd_new, gates,
            win_state, p, lam_init):
    nb, n_pages = page_table.shape
    npg = _PAGES_PER_STEP
    n_chunks = n_pages // npg
    rows_per_step = npg * PAGE_SIZE
    past_len = n_pages * PAGE_SIZE

    def page_spec(i):
        return pl.BlockSpec((None, None, PAGE_SIZE, KV_W),
                            lambda b, c, pt: (l, pt[b, c * npg + i], 0, 0))

    def per_seq(a):
        if a.ndim == 4:
            return pl.BlockSpec((None, None) + a.shape[2:], lambda b, c, pt: (l, b, 0, 0))
        return pl.BlockSpec((None,) + a.shape[1:], lambda b, c, pt: (b,) + (0,) * (a.ndim - 1))

    def const(a):
        return pl.BlockSpec(a.shape, lambda b, c, pt: (0,) * a.ndim)

    seq_ins = [qn16, qd16, selkv_new, winkv_new, kd_new, vd_new, gates, win_state]
    const_ins = [p["gmat"], p["wk_dec"], p["wv_dec"], p["pe_dec"], p["kcg"], p["selg"], p["wing"],
                 p["dkg"], p["lam"], p["subln"]]
    grid_spec = pltpu.PrefetchScalarGridSpec(
        num_scalar_prefetch=1,
        grid=(nb, n_chunks),
        in_specs=[page_spec(i) for i in range(npg)] + [per_seq(a) for a in seq_ins]
        + [const(a) for a in const_ins],
        out_specs=[pl.BlockSpec((None, 1, NSA_W), lambda b, c, pt: (b, 0, 0)),
                   pl.BlockSpec((None, 1, QD_W), lambda b, c, pt: (b, 0, 0))],
        scratch_shapes=[
            pltpu.VMEM((rows_per_step, LANES), F32),
            pltpu.VMEM((rows_per_step, DK_W), BF16),
            pltpu.VMEM((rows_per_step, DV_W), BF16),
            pltpu.VMEM((past_len, LANES), BF16),
            pltpu.VMEM((past_len // CMP_BLOCK, LANES), F32),
            pltpu.VMEM((_QROWS, 1), F32),
            pltpu.VMEM((_QROWS, 1), F32),
            pltpu.VMEM((_QROWS, DV_W), F32),
        ],
    )
    return pl.pallas_call(
        functools.partial(_decode_kernel, n_chunks=n_chunks, lam_init=lam_init),
        grid_spec=grid_spec,
        out_shape=[jax.ShapeDtypeStruct((nb, 1, NSA_W), BF16),
                   jax.ShapeDtypeStruct((nb, 1, QD_W), BF16)],
        compiler_params=_cparams(2),
        name="decode",
    )(page_table, *([cache] * npg), *seq_ins, *const_ins)


def _merge_kernel(oa_ref, ob_ref, gm_ref, x_ref, ga_ref, wa_ref, wb_ref, wo_ref, o_ref):
    d = x_ref.shape[-1]
    ya = _dot(oa_ref[...], wa_ref[...])
    yb = _dot(ob_ref[...], wb_ref[...])
    mg = gm_ref[:, 0:d] * ya + gm_ref[:, d:2 * d] * yb
    o_ref[...] = x_ref[...] + ga_ref[0] * _dot(mg.astype(BF16), wo_ref[...])


def _merge(oa, ob, gm, x2d, ga, tm, p):
    n, d = x2d.shape
    steps = n // tm
    r = ga.shape[1]
    steps_per_group = steps // ga.shape[0]

    def rows(width):
        return pl.BlockSpec((tm, width), lambda i: (i, 0))

    def const(a):
        return pl.BlockSpec(a.shape, lambda i: (0,) * a.ndim)

    return pl.pallas_call(
        _merge_kernel,
        grid=(steps,),
        in_specs=[rows(NSA_W), rows(QD_W), rows(2 * d), rows(d),
                  pl.BlockSpec((1, r, d), lambda i: (i // steps_per_group, 0, 0)),
                  const(p["w_br_nsa"]), const(p["w_br_diff"]), const(p["w_out"])],
        out_specs=rows(d),
        out_shape=jax.ShapeDtypeStruct((n, d), F32),
        compiler_params=_cparams(1),
        name="merge",
    )(oa, ob, gm, x2d, ga, p["w_br_nsa"], p["w_br_diff"], p["w_out"])


def _ffn_kernel(x_ref, sh_ref, sc_ref, ga_ref, g2_ref, wi_ref, wo_ref, o_ref):
    x = x_ref[...]
    d_ff = wo_ref.shape[0]
    ms = jnp.mean(x * x, axis=-1, keepdims=True)
    h = x * lax.rsqrt(ms + EPS) * g2_ref[...]
    hb = (h * (1.0 + sc_ref[0]) + sh_ref[0]).astype(BF16)
    u = _dot(hb, wi_ref[...])
    ug = u[:, 0:d_ff]
    act = (ug * jax.nn.sigmoid(ug) * u[:, d_ff:]).astype(BF16)
    o_ref[...] = x + ga_ref[0] * _dot(act, wo_ref[...])


def _ffn(x2d, sh, sc, ga, tm, p):
    n, d = x2d.shape
    steps = n // tm
    r = sh.shape[1]
    steps_per_group = steps // sh.shape[0]
    rows = pl.BlockSpec((tm, d), lambda i: (i, 0))
    mod_spec = pl.BlockSpec((1, r, d), lambda i: (i // steps_per_group, 0, 0))

    def const(a):
        return pl.BlockSpec(a.shape, lambda i: (0,) * a.ndim, pipeline_mode=pl.Buffered(1))

    return pl.pallas_call(
        _ffn_kernel,
        grid=(steps,),
        in_specs=[rows, mod_spec, mod_spec, mod_spec, const(p["g2"]), const(p["w_ffn_in"]),
                  const(p["w_ffn_out"])],
        out_specs=rows,
        out_shape=jax.ShapeDtypeStruct((n, d), F32),
        compiler_params=_cparams(1),
        name="ffn",
    )(x2d, sh, sc, ga, p["g2"], p["w_ffn_in"], p["w_ffn_out"])


def _tile_gain(g, reps, scale=1.0):
    return (jnp.tile(g.astype(F32), reps) * scale).reshape(1, -1)


def _half_gain(g):
    return jnp.concatenate([g.astype(F32), jnp.ones((HEAD_DIM,), F32)]).reshape(1, -1)


def _compress_matrix(w, n_blocks):
    return jnp.kron(jnp.eye(n_blocks, dtype=F32), w.astype(F32).reshape(1, -1))


def _prepare_params(l, tm_prompt, w_ada, b_ada, norm1_g, norm2_g, w_in, nsa_q_norm, nsa_k_norm,
                    nsa_cmp_wk, nsa_cmp_wv, nsa_cmp_pe, diff_q_norm, diff_k_norm, diff_lambda,
                    diff_subln, w_br_nsa, w_br_diff, w_out, w_ffn_in, w_ffn_out):
    d = w_in.shape[1]
    wi = w_in[l]
    o = [0, NSA_W, NSA_W + KV_W, NSA_W + KV_W + WIN_W, NSA_W + KV_W + WIN_W + GL_W,
         NSA_W + KV_W + WIN_W + GL_W + QD_W, NSA_W + KV_W + WIN_W + GL_W + QD_W + 2 * d]
    w_r = jnp.concatenate([wi[:, o[0]:o[1]], wi[:, o[1]:o[2]], wi[:, o[2]:o[3]], wi[:, o[4]:o[5]],
                           wi[:, o[5]:o[6]], wi[:, o[3]:o[4]],
                           jnp.zeros((d, LANES - GL_W), wi.dtype)], axis=1).astype(BF16)
    gidx = jnp.arange(NSA_W) // HEAD_DIM
    pe2 = jnp.concatenate([nsa_cmp_pe[l].astype(F32), jnp.zeros((CMP_BLOCK, HEAD_DIM), F32)], axis=1)
    nb_p = tm_prompt // CMP_BLOCK
    nb_d = _PAGES_PER_STEP * PAGE_SIZE // CMP_BLOCK
    return {
        "w_ada": w_ada[l], "b_ada": b_ada[l],
        "g1": norm1_g[l].astype(F32).reshape(1, -1), "g2": norm2_g[l].astype(F32).reshape(1, -1),
        "w_in": w_r,
        "gmat": (gidx[:, None] == gidx[None, :]).astype(BF16),
        "qg": _tile_gain(nsa_q_norm[l], NSA_HEADS, QK_SCALE),
        "dqg": _tile_gain(diff_q_norm[l], QD_W // HEAD_DIM, QK_SCALE),
        "kcg": _half_gain(nsa_k_norm[l, 0]),
        "selg": _half_gain(nsa_k_norm[l, 1]),
        "wing": _half_gain(nsa_k_norm[l, 2]),
        "dkg": _tile_gain(diff_k_norm[l], DK_W // HEAD_DIM),
        "wk": _compress_matrix(nsa_cmp_wk[l], nb_p), "wv": _compress_matrix(nsa_cmp_wv[l], nb_p),
        "pe": jnp.tile(pe2, (nb_p, 1)),
        "wk_dec": _compress_matrix(nsa_cmp_wk[l], nb_d), "wv_dec": _compress_matrix(nsa_cmp_wv[l], nb_d),
        "pe_dec": jnp.tile(pe2, (nb_d, 1)),
        "lam": diff_lambda[l].astype(F32), "subln": diff_subln[l].astype(F32).reshape(1, -1),
        "w_br_nsa": w_br_nsa[l].astype(BF16), "w_br_diff": w_br_diff[l].astype(BF16),
        "w_out": w_out[l].astype(BF16),
        "w_ffn_in": w_ffn_in[l].astype(BF16), "w_ffn_out": w_ffn_out[l].astype(BF16),
    }


def _pad_rows(a, rows):
    return jnp.concatenate([a, jnp.zeros((a.shape[0], rows - a.shape[1]) + a.shape[2:], a.dtype)], axis=1)


def _decode_queries(qn, qd):
    nb = qn.shape[0]
    qn16 = _pad_rows(qn.reshape(nb, NSA_HEADS, HEAD_DIM), _QROWS)
    q = qd.reshape(nb, DIFF_KV_HEADS, DIFF_GROUP, 2, HEAD_DIM)
    slot = jnp.arange(DIFF_KV_HEADS * 2).reshape(DIFF_KV_HEADS, 1, 2)
    onehot = (slot[..., None] == jnp.arange(DIFF_KV_HEADS * 2)).astype(q.dtype)
    qmat = q[:, :, :, :, None, :] * onehot[None, :, :, :, :, None]
    qd16 = _pad_rows(qmat.reshape(nb, DIFF_KV_HEADS * DIFF_GROUP * 2, DK_W), _QROWS)
    return qn16, qd16


def _layer(l, xp, xs, cache_kv, state_win_kv, page_table, mod, p, tm, tq):
    batch, seq, d = xp.shape
    nb = xs.shape[0]
    win_l = state_win_kv[l]
    lam_init = 0.8 - 0.6 * math.exp(-0.3 * l)
    mod_p = mod[:batch].reshape(batch, 1, 6 * d)
    mod_s = mod[batch:batch + nb].reshape(1, nb, 6 * d)

    def part(m, k):
        return m[:, :, k * d:(k + 1) * d]

    x2 = xp.reshape(batch * seq, d)
    (kv, win, qn, qd, selkv, winkv, kd, vd, gl, gm, kcvc) = _inproj(
        x2, part(mod_p, 0), part(mod_p, 1), tm, p, compress=True)
    oa = _nsa_prompt(qn, kcvc, selkv, winkv, gl, batch, seq, tq)
    ob = _diff_prompt(qd, kd, vd, p["lam"], p["subln"], batch, seq, tq, lam_init)
    x1 = _merge(oa, ob, gm, x2, part(mod_p, 2), tm, p)
    yp = _ffn(x1, part(mod_p, 3), part(mod_p, 4), part(mod_p, 5), tm, p)
    w_buf = win_l.shape[1]
    win_p = win.reshape(batch, seq, WIN_W)[:, seq - w_buf:]

    xs2 = xs.reshape(nb, d)
    (kv_s, win_s, qn_s, qd_s, selkv_s, winkv_s, kd_s, vd_s, gl_s, gm_s) = _inproj(
        xs2, part(mod_s, 0), part(mod_s, 1), nb, p, compress=False)
    qn16, qd16 = _decode_queries(qn_s, qd_s)
    oa_s, ob_s = _decode(l, page_table, cache_kv, qn16, qd16, selkv_s[:, None], winkv_s[:, None],
                         kd_s[:, None], vd_s[:, None], gl_s[:, None], state_win_kv, p, lam_init)
    x1s = _merge(oa_s.reshape(nb, NSA_W), ob_s.reshape(nb, QD_W), gm_s, xs2, part(mod_s, 2), nb, p)
    ys = _ffn(x1s, part(mod_s, 3), part(mod_s, 4), part(mod_s, 5), nb, p)
    win_state_s = jnp.concatenate([win_l[:, 1:], win_s[:, None]], axis=1)
    return (yp.reshape(batch, seq, d), ys.reshape(nb, 1, d), kv.reshape(batch, seq, KV_W),
            kv_s.reshape(nb, 1, KV_W), win_p, win_state_s)


def kernel(x_prompt, x_sample, cache_kv, state_win_kv, page_table, c_prompt, c_sample, w_ada, b_ada,
           norm1_g, norm2_g, w_in, nsa_q_norm, nsa_k_norm, nsa_cmp_wk, nsa_cmp_wv, nsa_cmp_pe,
           diff_q_norm, diff_k_norm, diff_lambda, diff_subln, w_br_nsa, w_br_diff, w_out, w_ffn_in,
           w_ffn_out):
    depth = w_in.shape[0]
    seq = x_prompt.shape[1]
    assert x_sample.shape[1] == 1 and cache_kv.shape[2] == PAGE_SIZE
    tm = min(512, seq)
    tq = 128
    xp, xs = x_prompt, x_sample
    kv_p, kv_s, win_p, win_s = [], [], [], []
    n_cond = c_prompt.shape[0] + c_sample.shape[0]
    c_all = jnp.concatenate([c_prompt, c_sample,
                             jnp.zeros((-n_cond % 16, c_prompt.shape[1]), c_prompt.dtype)], axis=0)
    for l in range(depth):
        p = _prepare_params(l, tm, w_ada, b_ada, norm1_g, norm2_g, w_in, nsa_q_norm, nsa_k_norm,
                            nsa_cmp_wk, nsa_cmp_wv, nsa_cmp_pe, diff_q_norm, diff_k_norm,
                            diff_lambda, diff_subln, w_br_nsa, w_br_diff, w_out, w_ffn_in, w_ffn_out)
        mod = _ada(c_all, p["w_ada"], p["b_ada"])
        xp, xs, kvp, kvs, wp, ws = _layer(l, xp, xs, cache_kv, state_win_kv, page_table, mod,
                                          p, tm, tq)
        kv_p.append(kvp)
        kv_s.append(kvs)
        win_p.append(wp)
        win_s.append(ws)
    return (xp, xs, jnp.stack(kv_p), jnp.stack(kv_s), jnp.stack(win_p), jnp.stack(win_s))
```

```python
import functools
import math

import jax
import jax.numpy as jnp
from jax import lax
from jax.experimental import pallas as pl
from jax.experimental.pallas import tpu as pltpu

F32 = jnp.float32
BF16 = jnp.bfloat16

HEAD_DIM = 64
NSA_HEADS = 8
CMP_BLOCK = 32
SEL_BLOCK = 64
TOP_N = 16
WINDOW = 512
DIFF_KV_HEADS = 2
DIFF_GROUP = 2
DIFF_VDIM = 2 * HEAD_DIM
PAGE_SIZE = 128
EPS = 1e-6
NEG_INF = -1e30
MASK_BIAS = -(2.0 ** 100)
FORCE_SCORE = 1e9
QK_SCALE = HEAD_DIM ** -0.5

LANES = 128
POS_SPLIT = 256
NSA_W = NSA_HEADS * HEAD_DIM
KV_W = 4 * HEAD_DIM + 2 * DIFF_KV_HEADS * 2 * HEAD_DIM
WIN_W = 2 * HEAD_DIM
GL_W = 3 * NSA_HEADS
N_QD = DIFF_KV_HEADS * DIFF_GROUP * 2
N_KD = DIFF_KV_HEADS * 2
QD_W = N_QD * HEAD_DIM
DK_W = N_KD * HEAD_DIM
DV_W = DIFF_KV_HEADS * DIFF_VDIM

VMEM_LIMIT_BYTES = 56 * 1024 * 1024


def _cparams(n_grid_dims):
    return pltpu.CompilerParams(
        dimension_semantics=("arbitrary",) * n_grid_dims,
        vmem_limit_bytes=VMEM_LIMIT_BYTES,
    )


def _dot(a, b):
    return jnp.dot(a, b, preferred_element_type=F32)


def _dot_nt(a, b):
    return lax.dot_general(a, b, (((1,), (1,)), ((), ())), preferred_element_type=F32)


def _split_dot(x, mat):
    hi = x.astype(BF16)
    lo = (x - hi.astype(F32)).astype(BF16)
    return _dot(hi, mat) + _dot(lo, mat)


def _group_sumsq(x, gmat):
    return _split_dot(x * x, gmat)


def _block_sumsq(x, gmat):
    w = x.shape[1]
    if w <= gmat.shape[0]:
        return _group_sumsq(x, gmat[0:w, 0:w])
    cw = gmat.shape[0]
    return jnp.concatenate([_group_sumsq(x[:, c:c + cw], gmat) for c in range(0, w, cw)], axis=1)


def _lane_iota(shape):
    return lax.broadcasted_iota(jnp.int32, shape, len(shape) - 1)


def _row_iota(shape):
    return lax.broadcasted_iota(jnp.int32, shape, 0)


def _alibi_slopes(n):
    return [2.0 ** (-8.0 * (i + 1) / n) for i in range(n)]


def _row_const(rows_per_group, values):
    n = len(values)
    row = _row_iota((n * rows_per_group, 1))
    out = jnp.full((n * rows_per_group, 1), values[0], F32)
    for g in range(1, n):
        out = jnp.where(row >= g * rows_per_group, values[g], out)
    return out


def _pos_lanes(pos, width):
    lane = _lane_iota((pos.shape[0], width)) & (LANES - 1)
    hi = (pos >> 8).astype(F32)
    lo = (pos & (POS_SPLIT - 1)).astype(F32)
    return jnp.where(lane == HEAD_DIM, hi, jnp.where(lane == HEAD_DIM + 1, lo, 0.0))


def _online_step(s, v, m, l, acc):
    m_new = jnp.maximum(m, jnp.max(s, axis=-1, keepdims=True))
    alpha = jnp.exp(m - m_new)
    p = jnp.exp(s - m_new)
    l = alpha * l + jnp.sum(p, axis=-1, keepdims=True)
    acc = alpha * acc + _dot(p.astype(BF16), v)
    return m_new, l, acc


def _flash_init(m_s, l_s, acc_s):
    m_s[...] = jnp.full(m_s.shape, NEG_INF, F32)
    l_s[...] = jnp.zeros(l_s.shape, F32)
    acc_s[...] = jnp.zeros(acc_s.shape, F32)


def _flash_update(s, v, m_s, l_s, acc_s, r0):
    nrows, tk = s.shape
    sl = slice(r0, r0 + nrows)
    cols = [s[:, c:c + LANES] for c in range(0, tk, LANES)]
    smax = cols[0]
    for c in cols[1:]:
        smax = jnp.maximum(smax, c)
    m_old = m_s[sl]
    m_new = jnp.maximum(m_old, jnp.max(smax, axis=-1, keepdims=True))
    alpha = jnp.exp(m_old - m_new)
    ps = [jnp.exp(c - m_new) for c in cols]
    psum = ps[0]
    for p in ps[1:]:
        psum = psum + p
    l_s[sl] = alpha * l_s[sl] + psum
    p_bf = jnp.concatenate([p.astype(BF16) for p in ps], axis=1)
    acc_s[sl] = alpha * acc_s[sl] + _dot(p_bf, v)
    m_s[sl] = m_new


def _flash_finish(m_s, l_s, acc_s):
    l = jnp.sum(l_s[...], axis=-1, keepdims=True)
    return jnp.where(m_s[...] > 0.5 * NEG_INF, acc_s[...] / jnp.maximum(l, 1e-30), 0.0)


def _flash_loop(qk, consume, last):
    qk(0, 0)

    def body(j, carry):
        kt = 2 * j
        qk(kt + 1, 1)
        consume(kt, 0, False)
        qk(kt + 2, 0)
        consume(kt + 1, 1, False)
        return carry

    lax.fori_loop(0, last >> 1, body, 0)

    @pl.when((last & 1) == 1)
    def _():
        qk(last, 1)
        consume(last - 1, 0, False)
        consume(last, 1, True)

    @pl.when((last & 1) == 0)
    def _():
        consume(last, 0, True)


def _finish(m, l, acc):
    return jnp.where(m > 0.5 * NEG_INF, acc / jnp.maximum(l, 1e-30), 0.0)


def _compress(x, wcol):
    r = x.shape[0]
    return jnp.sum((x * wcol).reshape(r // CMP_BLOCK, CMP_BLOCK, x.shape[1]), axis=1)


def _norm_compressed(raw, kcg):
    is_k = _lane_iota(raw.shape) < HEAD_DIM
    ss = jnp.sum(jnp.where(is_k, raw * raw, 0.0), axis=-1, keepdims=True)
    return jnp.where(is_k, raw * lax.rsqrt(ss * (1.0 / HEAD_DIM) + EPS) * kcg, raw)


def _ada_kernel(c_ref, w_ref, b_ref, o_ref):
    c = c_ref[...]
    sc = c * jax.nn.sigmoid(c)
    o_ref[...] = _dot(sc.astype(BF16), w_ref[...].astype(BF16)) + b_ref[...]


def _ada(c, w_ada, b_ada):
    m, d = c.shape
    n = w_ada.shape[1]
    tn = d
    return pl.pallas_call(
        _ada_kernel,
        grid=(n // tn,),
        in_specs=[
            pl.BlockSpec((m, d), lambda j: (0, 0)),
            pl.BlockSpec((d, tn), lambda j: (0, j)),
            pl.BlockSpec((1, tn), lambda j: (0, j)),
        ],
        out_specs=pl.BlockSpec((m, tn), lambda j: (0, j)),
        out_shape=jax.ShapeDtypeStruct((m, n), F32),
        compiler_params=_cparams(1),
        name="ada",
    )(c, w_ada, b_ada.reshape(1, n))


_C_QN = 0
_C_QD = _C_QN + NSA_HEADS * LANES
_C_KSEL = _C_QD + N_QD * LANES
_C_KWIN = _C_KSEL + LANES
_C_KD = _C_KWIN + LANES
_C_KV = _C_KD + N_KD * LANES
_C_WIN = _C_KV + KV_W
_C_MG = _C_WIN + WIN_W


def _inproj_kernel(*refs, compress, seq, steps_per_seq):
    (x_ref, sh_ref, sc_ref, g1_ref, w_ref, gmat_ref, qg_ref, qaug_ref, dqg_ref, dqaug_ref, selg_ref,
     wing_ref, dkg_ref) = refs[:13]
    n_in = 13
    if compress:
        wcol_ref, pe_ref, kcg_ref = refs[13:16]
        n_in = 16
    (kv_ref, win_ref, qn_ref, qd_ref, ksel_ref, kwin_ref, vnsa_ref, kd_ref, vd_ref, gl_ref,
     gm_ref) = refs[n_in:n_in + 11]
    tm, d_model = x_ref.shape
    c_gl = _C_MG + 2 * d_model

    x = x_ref[...]
    ms = jnp.mean(x * x, axis=-1, keepdims=True)
    h = x * lax.rsqrt(ms + EPS) * g1_ref[...]
    h = h * (1.0 + sc_ref[0]) + sh_ref[0]
    hb = h.astype(BF16)

    def proj(a, b):
        return _dot(hb, w_ref[:, a:b])

    def normed(z, gain):
        ss = _block_sumsq(z, gmat_ref[...])
        return z * lax.rsqrt(ss * (1.0 / HEAD_DIM) + EPS) * gain

    qn_ref[...] = (normed(proj(_C_QN, _C_QD), qg_ref[...]) + qaug_ref[...]).astype(BF16)
    qd_ref[...] = (normed(proj(_C_QD, _C_KSEL), dqg_ref[...]) + dqaug_ref[...]).astype(BF16)

    pos = (pl.program_id(0) % steps_per_seq) * tm + _row_iota((tm, 1))
    del seq
    ksel = normed(proj(_C_KSEL, _C_KWIN), selg_ref[...]) + _pos_lanes(pos, LANES)
    block_onehot = jnp.where(_lane_iota((tm, LANES)) == (pos >> 6), 1.0, 0.0)
    ksel_ref[...] = jnp.concatenate([ksel, block_onehot], axis=1).astype(BF16)
    kwin_ref[...] = (normed(proj(_C_KWIN, _C_KD), wing_ref[...]) + _pos_lanes(pos, LANES)).astype(BF16)
    kd_ref[...] = (normed(proj(_C_KD, _C_KV), dkg_ref[...])
                   + _pos_lanes(pos, N_KD * LANES)).astype(BF16)

    kv = proj(_C_KV, _C_WIN)
    kv_ref[...] = kv
    vd_ref[...] = kv[:, 4 * HEAD_DIM + DK_W:].astype(BF16)
    win = proj(_C_WIN, _C_MG)
    win_ref[...] = win
    sel_v_first = pltpu.roll(kv[:, 2 * HEAD_DIM:4 * HEAD_DIM], HEAD_DIM, axis=1)
    vnsa_ref[...] = jnp.where(_lane_iota((tm, LANES)) < HEAD_DIM, sel_v_first, win).astype(BF16)

    gm_ref[...] = jax.nn.sigmoid(proj(_C_MG, c_gl))
    gl_ref[...] = jax.nn.sigmoid(proj(c_gl, c_gl + LANES))

    if compress:
        kcvc_ref = refs[n_in + 11]
        raw = _compress(kv[:, 0:2 * HEAD_DIM] + pe_ref[...], wcol_ref[...])
        kcvc_ref[...] = _norm_compressed(raw, kcg_ref[...])


def _inproj(x2d, sh, sc, tm, p, seq, compress):
    n, d = x2d.shape
    steps = n // tm
    r = sh.shape[1]
    steps_per_group = steps // sh.shape[0]
    w = p["w_in"]

    def const(a):
        return pl.BlockSpec(a.shape, lambda i: (0,) * a.ndim, pipeline_mode=pl.Buffered(1))

    def rows(width):
        return pl.BlockSpec((tm, width), lambda i: (i, 0))

    mod_spec = pl.BlockSpec((1, r, d), lambda i: (i // steps_per_group, 0, 0))
    consts = [p["g1"], w, p["gmat128"], p["qg"], p["qaug"], p["dqg"], p["dqaug"], p["selg"],
              p["wing"], p["dkg"]]
    ins = [x2d, sh, sc] + consts
    in_specs = [rows(d), mod_spec, mod_spec] + [const(a) for a in consts]
    outs = [(KV_W, F32), (WIN_W, F32), (NSA_HEADS * LANES, BF16), (N_QD * LANES, BF16),
            (2 * LANES, BF16), (LANES, BF16), (LANES, BF16), (N_KD * LANES, BF16), (DV_W, BF16),
            (LANES, F32), (2 * d, F32)]
    out_shape = [jax.ShapeDtypeStruct((n, wd), dt) for wd, dt in outs]
    out_specs = [rows(wd) for wd, _ in outs]
    if compress:
        cins = [p["wcol"], p["pe"], p["kcg"]]
        ins += cins
        in_specs += [const(a) for a in cins]
        out_shape.append(jax.ShapeDtypeStruct((n // CMP_BLOCK, LANES), F32))
        out_specs.append(pl.BlockSpec((tm // CMP_BLOCK, LANES), lambda i: (i, 0)))
    return pl.pallas_call(
        functools.partial(_inproj_kernel, compress=compress, seq=seq,
                          steps_per_seq=max(seq // tm, 1)),
        grid=(steps,),
        in_specs=in_specs,
        out_specs=out_specs,
        out_shape=out_shape,
        compiler_params=_cparams(1),
        name="inproj_cmp" if compress else "inproj",
    )(*ins)


def _stack_blocks(q, n):
    return jnp.concatenate([q[:, h * LANES:(h + 1) * LANES] for h in range(n)], axis=0)


def _pair_sum(x, n_out):
    n2 = x.shape[1]
    r = _row_iota((n2, n_out))
    c = _lane_iota((n2, n_out))
    pmat = jnp.where((r >> 1) == c, 1.0, 0.0).astype(BF16)
    hi = x.astype(BF16)
    r1 = x - hi.astype(F32)
    mid = r1.astype(BF16)
    lo = (r1 - mid.astype(F32)).astype(BF16)
    return _dot(hi, pmat) + _dot(mid, pmat) + _dot(lo, pmat)


def _top_blocks(score, n_top):
    idx = _lane_iota(score.shape).astype(F32)
    big = float(score.shape[1])
    work, sel = score, jnp.zeros_like(score)
    for _ in range(n_top):
        mx = jnp.max(work, axis=-1, keepdims=True)
        first = jnp.min(jnp.where(work == mx, idx, big), axis=-1, keepdims=True)
        hit = idx == first
        work, sel = jnp.where(hit, -2.0, work), jnp.where(hit, 1.0, sel)
    return sel


def _rank_select(score_row, n_top):
    n = score_row.shape[1]
    r = jnp.broadcast_to(score_row, (n, n))
    c = r.T
    i = _row_iota((n, n))
    j = _lane_iota((n, n))
    beats = (c > r) | ((c == r) & (i < j))
    rank = jnp.sum(jnp.where(beats, 1.0, 0.0), axis=0, keepdims=True)
    return rank < n_top


def _expand_blocks(sel_bf, first_block, n_keys):
    nb = sel_bf.shape[1]
    b = _row_iota((nb, n_keys))
    j = _lane_iota((nb, n_keys))
    emat = jnp.where(b == first_block + (j >> 6), 1.0, 0.0).astype(BF16)
    return _dot(sel_bf, emat)


_ROW_CHUNK = 256


def _nsa_kernel(qn_ref, kcvc_ref, ksel_ref, kwin_ref, v_ref, gate_ref, o_ref, m_s, l_s, acc_s,
                s_buf, *, tq, tk, n_sel):
    qi = pl.program_id(1)
    nh = NSA_HEADS
    rows = nh * tq
    q0 = qi * tq
    qs = _stack_blocks(qn_ref[...], nh)
    qpos_t = q0 + _row_iota((tq, 1))

    def head(a, h):
        return a[h * tq:(h + 1) * tq]

    kcvc = kcvc_ref[...]
    n_cmp = kcvc.shape[0]
    cmp_end = _row_iota((n_cmp, 1)) * CMP_BLOCK + (CMP_BLOCK - 1)
    kc = jnp.where(_lane_iota(kcvc.shape) < HEAD_DIM, kcvc, _pos_lanes(cmp_end, LANES)).astype(BF16)
    ok_c = (_lane_iota((1, n_cmp)) * CMP_BLOCK + (CMP_BLOCK - 1)) <= qpos_t
    s_all = _dot_nt(qs, kc)
    p_heads = jnp.zeros((tq, n_cmp), F32)
    p_bf = []
    for h in range(nh):
        s = jnp.where(ok_c, head(s_all, h), NEG_INF)
        e = jnp.where(ok_c, jnp.exp(s - jnp.max(s, axis=-1, keepdims=True)), 0.0)
        p = e / jnp.maximum(jnp.sum(e, axis=-1, keepdims=True), 1e-30)
        p_heads = p_heads + p
        p_bf.append(p.astype(BF16))
    o_c = _dot(jnp.concatenate(p_bf, axis=0), kcvc.astype(BF16))
    imp = _pair_sum(p_heads, LANES)
    blk = _lane_iota((1, LANES))
    forced = (blk == (qpos_t >> 6)) | (blk == 0)
    valid = blk * SEL_BLOCK <= qpos_t
    score = jnp.where(valid, jnp.where(forced, FORCE_SCORE, imp), -1.0)

    span = tq + WINDOW
    start = pl.multiple_of(jnp.maximum(q0 - WINDOW, 0), tq)
    kw = kwin_ref[pl.ds(start, span), :]
    vw = v_ref[pl.ds(start, span), :]
    dist = qpos_t - (start + _lane_iota((1, span)))
    ok_w = (dist >= 0) & (dist < WINDOW)
    s_all = _dot_nt(qs, kw)
    p_bf, l_w = [], []
    for h in range(nh):
        s = jnp.where(ok_w, head(s_all, h), NEG_INF)
        e = jnp.exp(s - jnp.max(s, axis=-1, keepdims=True))
        l_w.append(jnp.sum(e, axis=-1, keepdims=True))
        p_bf.append(e.astype(BF16))
    o_w = _dot(jnp.concatenate(p_bf, axis=0), vw)

    sel = _top_blocks(score, min(TOP_N, n_sel))
    desel = jnp.where(valid & (sel > 0.5), 0.0, MASK_BIAS).astype(BF16)
    qfull = jnp.concatenate([qs, jnp.concatenate([desel] * nh, axis=0)], axis=1)

    _flash_init(m_s, l_s, acc_s)
    rc = min(_ROW_CHUNK, rows)

    def sel_qk(kt, slot):
        k = ksel_ref[pl.ds(pl.multiple_of(kt * tk, tk), tk), :]
        s_buf[slot] = _dot_nt(qfull, k)

    def sel_consume(kt, slot, causal):
        v = v_ref[pl.ds(pl.multiple_of(kt * tk, tk), tk), :]
        for r0 in range(0, rows, rc):
            s = s_buf[slot, r0:r0 + rc, :]
            if causal:
                qp = q0 + (_row_iota((rc, 1)) & (tq - 1))
                s = jnp.where(kt * tk + _lane_iota((1, tk)) <= qp, s, NEG_INF)
            _flash_update(s, v, m_s, l_s, acc_s, r0)

    _flash_loop(sel_qk, sel_consume, q0 // tk)
    o_s = _flash_finish(m_s, l_s, acc_s)

    gate = gate_ref[...]
    low = _lane_iota((tq, LANES)) < HEAD_DIM
    outs = []
    for h in range(nh):
        upper = gate[:, 3 * h:3 * h + 1] * head(o_c, h) + gate[:, 3 * h + 2:3 * h + 3] * (head(o_w, h) / l_w[h])
        both = jnp.where(low, gate[:, 3 * h + 1:3 * h + 2] * head(o_s, h), upper)
        outs.append(both + pltpu.roll(both, HEAD_DIM, axis=1))
    for j in range(nh // 2):
        o_ref[:, j * LANES:(j + 1) * LANES] = jnp.where(low, outs[2 * j], outs[2 * j + 1]).astype(o_ref.dtype)


def _nsa_prompt(qn, kcvc, ksel, kwin, vnsa, gates, batch, seq, tq, tk):
    nq = seq // tq
    n_cmp = seq // CMP_BLOCK
    n_sel = seq // SEL_BLOCK
    assert n_sel <= LANES and seq % tk == 0 and tk % tq == 0 and seq >= tq + WINDOW
    rows = NSA_HEADS * tq

    def qrow(width):
        return pl.BlockSpec((tq, width), lambda b, i: (b * nq + i, 0))

    def per_batch(rows_, width):
        return pl.BlockSpec((rows_, width), lambda b, i: (b, 0))

    return pl.pallas_call(
        functools.partial(_nsa_kernel, tq=tq, tk=tk, n_sel=n_sel),
        grid=(batch, nq),
        in_specs=[qrow(NSA_HEADS * LANES), per_batch(n_cmp, LANES), per_batch(seq, 2 * LANES),
                  per_batch(seq, LANES), per_batch(seq, LANES), qrow(LANES)],
        out_specs=qrow(NSA_W),
        out_shape=jax.ShapeDtypeStruct((batch * seq, NSA_W), BF16),
        scratch_shapes=[pltpu.VMEM((rows, LANES), F32), pltpu.VMEM((rows, LANES), F32),
                        pltpu.VMEM((rows, LANES), F32), pltpu.VMEM((2, rows, tk), F32)],
        compiler_params=_cparams(2),
        name="nsa_prompt",
    )(qn, kcvc, ksel, kwin, vnsa, gates)


def _lambda_full(lam_ref, lam_init):
    dl = lam_ref[...]
    a = jnp.sum(dl[0:1] * dl[1:2], axis=-1, keepdims=True)
    b = jnp.sum(dl[2:3] * dl[3:4], axis=-1, keepdims=True)
    return jnp.exp(a) - jnp.exp(b) + lam_init


def _subln(o, gain, lam_init):
    ms = jnp.mean(o * o, axis=-1, keepdims=True)
    return o * lax.rsqrt(ms + EPS) * gain * (1.0 - lam_init)


def _diff_kernel(q_ref, k_ref, v_ref, lam_ref, subln_ref, o_ref, m_s, l_s, acc_s, s_buf, *, tq, tk,
                 lam_init):
    qi = pl.program_id(2)
    g = DIFF_GROUP
    half = g * tq
    rows = 2 * half
    q0 = qi * tq
    q = q_ref[...]
    qm = [jnp.concatenate([q[:, (gg * 2 + m) * LANES:(gg * 2 + m + 1) * LANES] for gg in range(g)],
                          axis=0) for m in range(2)]
    _flash_init(m_s, l_s, acc_s)
    rc = min(_ROW_CHUNK, half)

    def qk(kt, slot):
        k = k_ref[pl.ds(pl.multiple_of(kt * tk, tk), tk), :]
        for m in range(2):
            s_buf[slot, m * half:(m + 1) * half, :] = _dot_nt(qm[m], k[:, m * LANES:(m + 1) * LANES])

    def consume(kt, slot, causal):
        v = v_ref[pl.ds(pl.multiple_of(kt * tk, tk), tk), :]
        for r0 in range(0, rows, rc):
            s = s_buf[slot, r0:r0 + rc, :]
            if causal:
                qp = q0 + (_row_iota((rc, 1)) & (tq - 1))
                s = jnp.where(kt * tk + _lane_iota((1, tk)) <= qp, s, NEG_INF)
            _flash_update(s, v, m_s, l_s, acc_s, r0)

    _flash_loop(qk, consume, q0 // tk)
    o = _flash_finish(m_s, l_s, acc_s)
    lam = _lambda_full(lam_ref, lam_init)
    for gg in range(g):
        r0 = gg * tq
        d = o[r0:r0 + tq] - lam * o[half + r0:half + r0 + tq]
        o_ref[:, gg * DIFF_VDIM:(gg + 1) * DIFF_VDIM] = _subln(d, subln_ref[...], lam_init).astype(o_ref.dtype)


def _diff_prompt(qd, kd, vd, lam, subln, batch, seq, tq, tk, lam_init):
    nq = seq // tq
    assert seq % tk == 0 and tk % tq == 0
    rows = 2 * DIFF_GROUP * tq
    return pl.pallas_call(
        functools.partial(_diff_kernel, tq=tq, tk=tk, lam_init=lam_init),
        grid=(batch, DIFF_KV_HEADS, nq),
        in_specs=[
            pl.BlockSpec((tq, DIFF_GROUP * 2 * LANES), lambda b, n, i: (b * nq + i, n)),
            pl.BlockSpec((seq, 2 * LANES), lambda b, n, i: (b, n)),
            pl.BlockSpec((seq, DIFF_VDIM), lambda b, n, i: (b, n)),
            pl.BlockSpec(lam.shape, lambda b, n, i: (0, 0)),
            pl.BlockSpec(subln.shape, lambda b, n, i: (0, 0)),
        ],
        out_specs=pl.BlockSpec((tq, DIFF_GROUP * DIFF_VDIM), lambda b, n, i: (b * nq + i, n)),
        out_shape=jax.ShapeDtypeStruct((batch * seq, QD_W), BF16),
        scratch_shapes=[pltpu.VMEM((rows, LANES), F32), pltpu.VMEM((rows, LANES), F32),
                        pltpu.VMEM((rows, DIFF_VDIM), F32), pltpu.VMEM((2, rows, tk), F32)],
        compiler_params=_cparams(3),
        name="diff_prompt",
    )(qd, kd, vd, lam, subln)


_QROWS = 16
_PAGES_PER_STEP = 16


def _decode_kernel(pt_ref, *refs, n_chunks, lam_init):
    del pt_ref
    npg = _PAGES_PER_STEP
    pages = refs[:npg]
    (qn_ref, qd_ref, selkv_new_ref, winkv_new_ref, kd_new_ref, vd_new_ref, gate_ref, win_ref,
     gmat_ref, wcol_ref, pe_ref, kcg_ref, selg_ref, wing_ref, dkg_ref, lam_ref,
     subln_ref) = refs[npg:npg + 17]
    oa_ref, ob_ref = refs[npg + 17:npg + 19]
    (cmp_buf, kd_buf, vd_buf, stash, sel_buf, kcvc_s, dm_s, dl_s, dacc_s) = refs[npg + 19:]
    c = pl.program_id(1)
    rows_per_step = npg * PAGE_SIZE
    past_len = n_chunks * rows_per_step
    nh = NSA_HEADS

    for i in range(npg):
        pg = pages[i]
        r0 = i * PAGE_SIZE
        cmp_buf[r0:r0 + PAGE_SIZE, :] = pg[:, 0:2 * HEAD_DIM]
        sk = pg[:, 2 * HEAD_DIM:4 * HEAD_DIM]
        ss = _group_sumsq(sk, gmat_ref[0:LANES, 0:LANES])
        r = jnp.where(_lane_iota(sk.shape) < HEAD_DIM, lax.rsqrt(ss * (1.0 / HEAD_DIM) + EPS), 1.0)
        base = pl.multiple_of(c * rows_per_step + r0, PAGE_SIZE)
        stash[pl.ds(base, PAGE_SIZE), :] = (sk * r * selg_ref[...]).astype(BF16)
        dk = pg[:, 4 * HEAD_DIM:4 * HEAD_DIM + DK_W]
        ss = _group_sumsq(dk, gmat_ref[0:DK_W, 0:DK_W])
        kd_buf[r0:r0 + PAGE_SIZE, :] = (dk * lax.rsqrt(ss * (1.0 / HEAD_DIM) + EPS)
                                        * dkg_ref[...]).astype(BF16)
        vd_buf[r0:r0 + PAGE_SIZE, :] = pg[:, 4 * HEAD_DIM + DK_W:].astype(BF16)

    raw = _compress(cmp_buf[...] + pe_ref[...], wcol_ref[...])
    cmp_rows = rows_per_step // CMP_BLOCK
    kcvc_s[pl.ds(pl.multiple_of(c * cmp_rows, cmp_rows), cmp_rows), :] = _norm_compressed(raw, kcg_ref[...])

    slopes = _alibi_slopes(DIFF_KV_HEADS * DIFF_GROUP)
    dslope = _row_const(1, [slopes[r // 2] for r in range(8)] + [0.0] * (_QROWS - 8))
    qd = qd_ref[...]

    @pl.when(c == 0)
    def _():
        dm_s[...] = jnp.full(dm_s.shape, NEG_INF, F32)
        dl_s[...] = jnp.zeros(dl_s.shape, F32)
        dacc_s[...] = jnp.zeros(dacc_s.shape, F32)

    kpos = c * rows_per_step + _lane_iota((1, rows_per_step))
    sc = _dot_nt(qd, kd_buf[...]) - dslope * (past_len - kpos).astype(F32)
    dm, dl, dacc = _online_step(sc, vd_buf[...], dm_s[...], dl_s[...], dacc_s[...])
    dm_s[...] = dm
    dl_s[...] = dl
    dacc_s[...] = dacc

    @pl.when(c == n_chunks - 1)
    def _():
        kd_new = kd_new_ref[...].astype(F32)
        s_new = jnp.sum(qd.astype(F32) * kd_new, axis=-1, keepdims=True)
        m2 = jnp.maximum(dm, s_new)
        alpha = jnp.exp(dm - m2)
        p_new = jnp.exp(s_new - m2)
        l2 = alpha * dl + p_new
        acc2 = alpha * dacc + p_new * vd_new_ref[...].astype(F32)
        od = acc2 / l2
        lam = _lambda_full(lam_ref, lam_init)
        for n in range(DIFF_KV_HEADS):
            for g in range(DIFF_GROUP):
                r = (n * DIFF_GROUP + g) * 2
                lo, hi = n * DIFF_VDIM, (n + 1) * DIFF_VDIM
                d = od[r:r + 1, lo:hi] - lam * od[r + 1:r + 2, lo:hi]
                col = (n * DIFF_GROUP + g) * DIFF_VDIM
                ob_ref[:, col:col + DIFF_VDIM] = _subln(d, subln_ref[...], lam_init).astype(ob_ref.dtype)

        qn = qn_ref[...]
        aslope = _row_const(1, _alibi_slopes(nh) + [0.0] * (_QROWS - nh))
        kcvc = kcvc_s[...]
        n_cmp = kcvc.shape[0]
        cmp_end = _lane_iota((1, n_cmp)) * CMP_BLOCK + (CMP_BLOCK - 1)
        s = _dot_nt(qn, kcvc[:, :HEAD_DIM].astype(BF16)) - aslope * (past_len - cmp_end).astype(F32)
        e = jnp.exp(s - jnp.max(s, axis=-1, keepdims=True))
        p_c = e / jnp.sum(e, axis=-1, keepdims=True)
        o_c = _dot(p_c.astype(BF16), kcvc[:, HEAD_DIM:].astype(BF16))
        head_row = _row_iota((_QROWS, 1)) < nh
        p_sum = jnp.sum(jnp.where(head_row, p_c, 0.0), axis=0, keepdims=True)
        n_blk = n_cmp // 2
        imp = _pair_sum(jnp.broadcast_to(p_sum, (8, n_cmp)), LANES)[0:1]
        blk = _lane_iota((1, LANES))
        score = jnp.where(blk == 0, FORCE_SCORE, jnp.where(blk < n_blk, imp, -1.0))
        n_pick = min(TOP_N - 1, n_blk)
        sel = _rank_select(score, n_pick)
        r = _row_iota((LANES, LANES))
        upper = jnp.where(r < _lane_iota((LANES, LANES)), 1.0, 0.0).astype(BF16)
        sel8 = jnp.broadcast_to(jnp.where(sel, 1.0, 0.0), (8, LANES)).astype(BF16)
        slot_of = _dot(sel8, upper)[0:1]
        slot = _row_iota((_QROWS, 1)).astype(F32)
        block_of = jnp.sum(jnp.where(sel & (slot_of == slot), blk.astype(F32), 0.0),
                           axis=-1, keepdims=True).astype(jnp.int32)
        lane_slot = _lane_iota((1, _QROWS * SEL_BLOCK)) >> 6
        kpos = _lane_iota((1, _QROWS * SEL_BLOCK)) & (SEL_BLOCK - 1)
        for t in range(n_pick):
            b_t = block_of[t, 0]
            src = pl.multiple_of(b_t * SEL_BLOCK, SEL_BLOCK)
            sel_buf[t * SEL_BLOCK:(t + 1) * SEL_BLOCK, :] = stash[pl.ds(src, SEL_BLOCK), :]
            kpos = jnp.where(lane_slot == t, kpos + b_t * SEL_BLOCK, kpos)
        sel_buf[n_pick * SEL_BLOCK:, :] = jnp.zeros(((_QROWS - n_pick) * SEL_BLOCK, LANES), BF16)
        kv = sel_buf[...]
        sc = _dot_nt(qn, kv[:, :HEAD_DIM]) - aslope * (past_len - kpos).astype(F32)
        sc = jnp.where(lane_slot < n_pick, sc, NEG_INF)

        init = (jnp.full((_QROWS, 1), NEG_INF, F32), jnp.zeros((_QROWS, 1), F32),
                jnp.zeros((_QROWS, HEAD_DIM), F32))
        sm, sl_, sacc = _online_step(sc, kv[:, HEAD_DIM:], *init)
        qf = qn.astype(F32)

        def add_new(m, l, acc, kv_new):
            s_new = jnp.sum(qf * kv_new[:, :HEAD_DIM], axis=-1, keepdims=True)
            m2 = jnp.maximum(m, s_new)
            alpha = jnp.exp(m - m2)
            p_new = jnp.exp(s_new - m2)
            return (alpha * acc + p_new * kv_new[:, HEAD_DIM:]) / (alpha * l + p_new)

        o_s = add_new(sm, sl_, sacc, selkv_new_ref[...].astype(F32))

        win = win_ref[...]
        w_buf = win.shape[0]
        ss = _group_sumsq(win, gmat_ref[0:LANES, 0:LANES])
        r = jnp.where(_lane_iota(win.shape) < HEAD_DIM, lax.rsqrt(ss * (1.0 / HEAD_DIM) + EPS), 1.0)
        wkv = (win * r * wing_ref[...]).astype(BF16)
        dist_w = w_buf - _lane_iota((1, w_buf))
        sc = _dot_nt(qn, wkv[:, :HEAD_DIM]) - aslope * dist_w.astype(F32)
        sc = jnp.where(dist_w < WINDOW, sc, NEG_INF)
        o_w = add_new(*_online_step(sc, wkv[:, HEAD_DIM:], *init), winkv_new_ref[...].astype(F32))

        gate = gate_ref[...]
        for h in range(nh):
            o = (gate[:, 3 * h:3 * h + 1] * o_c[h:h + 1] + gate[:, 3 * h + 1:3 * h + 2] * o_s[h:h + 1]
                 + gate[:, 3 * h + 2:3 * h + 3] * o_w[h:h + 1])
            oa_ref[:, h * HEAD_DIM:(h + 1) * HEAD_DIM] = o.astype(oa_ref.dtype)


def _decode(l, page_table, cache, qn16, qd16, selkv_new, winkv_new, kd_new, vd_new, gates,
            win_state, p, lam_init):
    nb, n_pages = page_table.shape
    npg = _PAGES_PER_STEP
    n_chunks = n_pages // npg
    rows_per_step = npg * PAGE_SIZE
    past_len = n_pages * PAGE_SIZE

    def page_spec(i):
        return pl.BlockSpec((None, None, PAGE_SIZE, KV_W),
                            lambda b, c, pt: (l, pt[b, c * npg + i], 0, 0))

    def per_seq(a):
        if a.ndim == 4:
            return pl.BlockSpec((None, None) + a.shape[2:], lambda b, c, pt: (l, b, 0, 0))
        return pl.BlockSpec((None,) + a.shape[1:], lambda b, c, pt: (b,) + (0,) * (a.ndim - 1))

    def const(a):
        return pl.BlockSpec(a.shape, lambda b, c, pt: (0,) * a.ndim)

    seq_ins = [qn16, qd16, selkv_new, winkv_new, kd_new, vd_new, gates, win_state]
    const_ins = [p["gmat64"], p["wcol_dec"], p["pe_dec"], p["kcg"], p["selg64"], p["wing64"],
                 p["dkg64"], p["lam"], p["subln"]]
    grid_spec = pltpu.PrefetchScalarGridSpec(
        num_scalar_prefetch=1,
        grid=(nb, n_chunks),
        in_specs=[page_spec(i) for i in range(npg)] + [per_seq(a) for a in seq_ins]
        + [const(a) for a in const_ins],
        out_specs=[pl.BlockSpec((None, 1, NSA_W), lambda b, c, pt: (b, 0, 0)),
                   pl.BlockSpec((None, 1, QD_W), lambda b, c, pt: (b, 0, 0))],
        scratch_shapes=[
            pltpu.VMEM((rows_per_step, LANES), F32),
            pltpu.VMEM((rows_per_step, DK_W), BF16),
            pltpu.VMEM((rows_per_step, DV_W), BF16),
            pltpu.VMEM((past_len, LANES), BF16),
            pltpu.VMEM((_QROWS * SEL_BLOCK, LANES), BF16),
            pltpu.VMEM((past_len // CMP_BLOCK, LANES), F32),
            pltpu.VMEM((_QROWS, 1), F32),
            pltpu.VMEM((_QROWS, 1), F32),
            pltpu.VMEM((_QROWS, DV_W), F32),
        ],
    )
    return pl.pallas_call(
        functools.partial(_decode_kernel, n_chunks=n_chunks, lam_init=lam_init),
        grid_spec=grid_spec,
        out_shape=[jax.ShapeDtypeStruct((nb, 1, NSA_W), BF16),
                   jax.ShapeDtypeStruct((nb, 1, QD_W), BF16)],
        compiler_params=_cparams(2),
        name="decode",
    )(page_table, *([cache] * npg), *seq_ins, *const_ins)


def _merge_kernel(oa_ref, ob_ref, gm_ref, x_ref, ga_ref, wa_ref, wb_ref, wo_ref, o_ref):
    d = x_ref.shape[-1]
    ya = _dot(oa_ref[...], wa_ref[...])
    yb = _dot(ob_ref[...], wb_ref[...])
    mg = gm_ref[:, 0:d] * ya + gm_ref[:, d:2 * d] * yb
    o_ref[...] = x_ref[...] + ga_ref[0] * _dot(mg.astype(BF16), wo_ref[...])


def _merge(oa, ob, gm, x2d, ga, tm, p):
    n, d = x2d.shape
    steps = n // tm
    r = ga.shape[1]
    steps_per_group = steps // ga.shape[0]

    def rows(width):
        return pl.BlockSpec((tm, width), lambda i: (i, 0))

    def const(a):
        return pl.BlockSpec(a.shape, lambda i: (0,) * a.ndim)

    return pl.pallas_call(
        _merge_kernel,
        grid=(steps,),
        in_specs=[rows(NSA_W), rows(QD_W), rows(2 * d), rows(d),
                  pl.BlockSpec((1, r, d), lambda i: (i // steps_per_group, 0, 0)),
                  const(p["w_br_nsa"]), const(p["w_br_diff"]), const(p["w_out"])],
        out_specs=rows(d),
        out_shape=jax.ShapeDtypeStruct((n, d), F32),
        compiler_params=_cparams(1),
        name="merge",
    )(oa, ob, gm, x2d, ga, p["w_br_nsa"], p["w_br_diff"], p["w_out"])


def _ffn_kernel(x_ref, sh_ref, sc_ref, ga_ref, g2_ref, wi_ref, wo_ref, o_ref):
    x = x_ref[...]
    d_ff = wo_ref.shape[0]
    ms = jnp.mean(x * x, axis=-1, keepdims=True)
    h = x * lax.rsqrt(ms + EPS) * g2_ref[...]
    hb = (h * (1.0 + sc_ref[0]) + sh_ref[0]).astype(BF16)
    u = _dot(hb, wi_ref[...])
    ug = u[:, 0:d_ff]
    act = (ug * jax.nn.sigmoid(ug) * u[:, d_ff:]).astype(BF16)
    o_ref[...] = x + ga_ref[0] * _dot(act, wo_ref[...])


def _ffn(x2d, sh, sc, ga, tm, p):
    n, d = x2d.shape
    steps = n // tm
    r = sh.shape[1]
    steps_per_group = steps // sh.shape[0]
    rows = pl.BlockSpec((tm, d), lambda i: (i, 0))
    mod_spec = pl.BlockSpec((1, r, d), lambda i: (i // steps_per_group, 0, 0))

    def const(a):
        return pl.BlockSpec(a.shape, lambda i: (0,) * a.ndim, pipeline_mode=pl.Buffered(1))

    return pl.pallas_call(
        _ffn_kernel,
        grid=(steps,),
        in_specs=[rows, mod_spec, mod_spec, mod_spec, const(p["g2"]), const(p["w_ffn_in"]),
                  const(p["w_ffn_out"])],
        out_specs=rows,
        out_shape=jax.ShapeDtypeStruct((n, d), F32),
        compiler_params=_cparams(1),
        name="ffn",
    )(x2d, sh, sc, ga, p["g2"], p["w_ffn_in"], p["w_ffn_out"])


def _pad_blocks(a, n):
    lead = a.shape[:-1]
    a = a.reshape(lead + (n, HEAD_DIM))
    a = jnp.concatenate([a, jnp.zeros_like(a)], axis=-1)
    return a.reshape(lead + (n * LANES,))


def _block_gain(g, reps, scale=1.0):
    return _pad_blocks(jnp.tile(g.astype(F32), reps) * scale, reps).reshape(1, -1)


def _slope_lanes(slopes):
    out = jnp.zeros((len(slopes), LANES), F32)
    s = jnp.asarray(slopes, F32)
    out = out.at[:, HEAD_DIM].set(s * POS_SPLIT).at[:, HEAD_DIM + 1].set(s)
    return out.reshape(1, -1)


def _half_gain(g):
    return jnp.concatenate([g.astype(F32), jnp.ones((HEAD_DIM,), F32)]).reshape(1, -1)


def _compress_weights(wk, wv, rows):
    col = jnp.concatenate([jnp.tile(wk.astype(F32)[:, None], (1, HEAD_DIM)),
                           jnp.tile(wv.astype(F32)[:, None], (1, HEAD_DIM))], axis=1)
    return jnp.tile(col, (rows // CMP_BLOCK, 1))


def _prepare_params(l, tm_prompt, w_ada, b_ada, norm1_g, norm2_g, w_in, nsa_q_norm, nsa_k_norm,
                    nsa_cmp_wk, nsa_cmp_wv, nsa_cmp_pe, diff_q_norm, diff_k_norm, diff_lambda,
                    diff_subln, w_br_nsa, w_br_diff, w_out, w_ffn_in, w_ffn_out):
    d = w_in.shape[1]
    wi = w_in[l]
    o_kv = NSA_W
    o_win = o_kv + KV_W
    o_gl = o_win + WIN_W
    o_qd = o_gl + GL_W
    o_mg = o_qd + QD_W
    w_kv = wi[:, o_kv:o_win]
    w_win = wi[:, o_win:o_gl]
    w_r = jnp.concatenate([
        _pad_blocks(wi[:, 0:NSA_W], NSA_HEADS),
        _pad_blocks(wi[:, o_qd:o_mg], N_QD),
        _pad_blocks(w_kv[:, 2 * HEAD_DIM:3 * HEAD_DIM], 1),
        _pad_blocks(w_win[:, 0:HEAD_DIM], 1),
        _pad_blocks(w_kv[:, 4 * HEAD_DIM:4 * HEAD_DIM + DK_W], N_KD),
        w_kv, w_win, wi[:, o_mg:o_mg + 2 * d], wi[:, o_gl:o_qd],
        jnp.zeros((d, LANES - GL_W), wi.dtype)], axis=1).astype(BF16)
    pe2 = jnp.concatenate([nsa_cmp_pe[l].astype(F32), jnp.zeros((CMP_BLOCK, HEAD_DIM), F32)], axis=1)
    dec_rows = _PAGES_PER_STEP * PAGE_SIZE
    nsa_slopes = _alibi_slopes(NSA_HEADS)
    diff_slopes = _alibi_slopes(DIFF_KV_HEADS * DIFF_GROUP)
    g64 = jnp.arange(2 * LANES) // HEAD_DIM
    g128 = jnp.arange(2 * LANES) // LANES
    return {
        "w_ada": w_ada[l], "b_ada": b_ada[l],
        "g1": norm1_g[l].astype(F32).reshape(1, -1), "g2": norm2_g[l].astype(F32).reshape(1, -1),
        "w_in": w_r,
        "gmat64": (g64[:, None] == g64[None, :]).astype(BF16),
        "gmat128": (g128[:, None] == g128[None, :]).astype(BF16),
        "qg": _block_gain(nsa_q_norm[l], NSA_HEADS, QK_SCALE),
        "qaug": _slope_lanes(nsa_slopes),
        "dqg": _block_gain(diff_q_norm[l], N_QD, QK_SCALE),
        "dqaug": _slope_lanes([diff_slopes[j // 2] for j in range(N_QD)]),
        "selg": _block_gain(nsa_k_norm[l, 1], 1), "wing": _block_gain(nsa_k_norm[l, 2], 1),
        "dkg": _block_gain(diff_k_norm[l], N_KD),
        "kcg": _half_gain(nsa_k_norm[l, 0]),
        "selg64": _half_gain(nsa_k_norm[l, 1]), "wing64": _half_gain(nsa_k_norm[l, 2]),
        "dkg64": jnp.tile(diff_k_norm[l].astype(F32), N_KD).reshape(1, -1),
        "wcol": _compress_weights(nsa_cmp_wk[l], nsa_cmp_wv[l], tm_prompt),
        "pe": jnp.tile(pe2, (tm_prompt // CMP_BLOCK, 1)),
        "wcol_dec": _compress_weights(nsa_cmp_wk[l], nsa_cmp_wv[l], dec_rows),
        "pe_dec": jnp.tile(pe2, (dec_rows // CMP_BLOCK, 1)),
        "lam": diff_lambda[l].astype(F32), "subln": diff_subln[l].astype(F32).reshape(1, -1),
        "w_br_nsa": w_br_nsa[l].astype(BF16), "w_br_diff": w_br_diff[l].astype(BF16),
        "w_out": w_out[l].astype(BF16),
        "w_ffn_in": w_ffn_in[l].astype(BF16), "w_ffn_out": w_ffn_out[l].astype(BF16),
    }


def _pad_rows(a, rows):
    return jnp.concatenate([a, jnp.zeros((a.shape[0], rows - a.shape[1]) + a.shape[2:], a.dtype)], axis=1)


def _unblock(a, n):
    return a.reshape(a.shape[0], n, LANES)[:, :, :HEAD_DIM]


def _decode_queries(qn, qd):
    nb = qn.shape[0]
    qn16 = _pad_rows(qn, _QROWS)
    q = qd.reshape(nb, DIFF_KV_HEADS, DIFF_GROUP, 2, HEAD_DIM)
    slot = jnp.arange(N_KD).reshape(DIFF_KV_HEADS, 1, 2)
    onehot = (slot[..., None] == jnp.arange(N_KD)).astype(q.dtype)
    qmat = q[:, :, :, :, None, :] * onehot[None, :, :, :, :, None]
    qd16 = _pad_rows(qmat.reshape(nb, N_QD, DK_W), _QROWS)
    return qn16, qd16


def _layer(l, xp, xs, cache_kv, state_win_kv, page_table, mod, p, tm, tiles):
    batch, seq, d = xp.shape
    nb = xs.shape[0]
    win_l = state_win_kv[l]
    lam_init = 0.8 - 0.6 * math.exp(-0.3 * l)
    mod_p = mod[:batch].reshape(batch, 1, 6 * d)
    mod_s = mod[batch:batch + nb].reshape(1, nb, 6 * d)

    def part(m, k):
        return m[:, :, k * d:(k + 1) * d]

    x2 = xp.reshape(batch * seq, d)
    (kv, win, qn, qd, ksel, kwin, vnsa, kd, vd, gl, gm, kcvc) = _inproj(
        x2, part(mod_p, 0), part(mod_p, 1), tm, p, seq, compress=True)
    oa = _nsa_prompt(qn, kcvc, ksel, kwin, vnsa, gl, batch, seq, tiles["nsa_tq"], tiles["nsa_tk"])
    ob = _diff_prompt(qd, kd, vd, p["lam"], p["subln"], batch, seq, tiles["diff_tq"],
                      tiles["diff_tk"], lam_init)
    x1 = _merge(oa, ob, gm, x2, part(mod_p, 2), tm, p)
    yp = _ffn(x1, part(mod_p, 3), part(mod_p, 4), part(mod_p, 5), tm, p)
    w_buf = win_l.shape[1]
    win_p = win.reshape(batch, seq, WIN_W)[:, seq - w_buf:]

    xs2 = xs.reshape(nb, d)
    (kv_s, win_s, qn_s, qd_s, ksel_s, kwin_s, vnsa_s, kd_s, vd_s, gl_s, gm_s) = _inproj(
        xs2, part(mod_s, 0), part(mod_s, 1), nb, p, 1, compress=False)
    qn16, qd16 = _decode_queries(_unblock(qn_s, NSA_HEADS), _unblock(qd_s, N_QD))
    selkv_s = jnp.concatenate([ksel_s[:, :HEAD_DIM], vnsa_s[:, :HEAD_DIM]], axis=1)
    winkv_s = jnp.concatenate([kwin_s[:, :HEAD_DIM], vnsa_s[:, HEAD_DIM:]], axis=1)
    kd_plain = _unblock(kd_s, N_KD).reshape(nb, DK_W)
    oa_s, ob_s = _decode(l, page_table, cache_kv, qn16, qd16, selkv_s[:, None], winkv_s[:, None],
                         kd_plain[:, None], vd_s[:, None], gl_s[:, None], state_win_kv, p, lam_init)
    x1s = _merge(oa_s.reshape(nb, NSA_W), ob_s.reshape(nb, QD_W), gm_s, xs2, part(mod_s, 2), nb, p)
    ys = _ffn(x1s, part(mod_s, 3), part(mod_s, 4), part(mod_s, 5), nb, p)
    win_state_s = jnp.concatenate([win_l[:, 1:], win_s[:, None]], axis=1)
    return (yp.reshape(batch, seq, d), ys.reshape(nb, 1, d), kv.reshape(batch, seq, KV_W),
            kv_s.reshape(nb, 1, KV_W), win_p, win_state_s)


def _tiles(seq):
    return {"nsa_tq": 128, "nsa_tk": min(512, seq), "diff_tq": min(256, seq), "diff_tk": min(512, seq)}


def kernel(x_prompt, x_sample, cache_kv, state_win_kv, page_table, c_prompt, c_sample, w_ada, b_ada,
           norm1_g, norm2_g, w_in, nsa_q_norm, nsa_k_norm, nsa_cmp_wk, nsa_cmp_wv, nsa_cmp_pe,
           diff_q_norm, diff_k_norm, diff_lambda, diff_subln, w_br_nsa, w_br_diff, w_out, w_ffn_in,
           w_ffn_out):
    depth = w_in.shape[0]
    seq = x_prompt.shape[1]
    assert x_sample.shape[1] == 1 and cache_kv.shape[2] == PAGE_SIZE
    tm = min(512, seq)
    xp, xs = x_prompt, x_sample
    kv_p, kv_s, win_p, win_s = [], [], [], []
    n_cond = c_prompt.shape[0] + c_sample.shape[0]
    c_all = jnp.concatenate([c_prompt, c_sample,
                             jnp.zeros((-n_cond % 16, c_prompt.shape[1]), c_prompt.dtype)], axis=0)
    for l in range(depth):
        p = _prepare_params(l, tm, w_ada, b_ada, norm1_g, norm2_g, w_in, nsa_q_norm, nsa_k_norm,
                            nsa_cmp_wk, nsa_cmp_wv, nsa_cmp_pe, diff_q_norm, diff_k_norm,
                            diff_lambda, diff_subln, w_br_nsa, w_br_diff, w_out, w_ffn_in, w_ffn_out)
        mod = _ada(c_all, p["w_ada"], p["b_ada"])
        xp, xs, kvp, kvs, wp, ws = _layer(l, xp, xs, cache_kv, state_win_kv, page_table, mod, p, tm,
                                          _tiles(seq))
        kv_p.append(kvp)
        kv_s.append(kvs)
        win_p.append(wp)
        win_s.append(ws)
    return (xp, xs, jnp.stack(kv_p), jnp.stack(kv_s), jnp.stack(win_p), jnp.stack(win_s))
```

```python
import functools
import math

import jax
import jax.numpy as jnp
from jax import lax
from jax.experimental import pallas as pl
from jax.experimental.pallas import tpu as pltpu

F32 = jnp.float32
BF16 = jnp.bfloat16

HEAD_DIM = 64
NSA_HEADS = 8
CMP_BLOCK = 32
SEL_BLOCK = 64
TOP_N = 16
WINDOW = 512
DIFF_KV_HEADS = 2
DIFF_GROUP = 2
DIFF_VDIM = 2 * HEAD_DIM
PAGE_SIZE = 128
EPS = 1e-6
NEG_INF = -1e30
MASK_BIAS = -(2.0 ** 100)
FORCE_SCORE = 1e9
QK_SCALE = HEAD_DIM ** -0.5

LANES = 128
POS_SPLIT = 256
NSA_W = NSA_HEADS * HEAD_DIM
KV_W = 4 * HEAD_DIM + 2 * DIFF_KV_HEADS * 2 * HEAD_DIM
WIN_W = 2 * HEAD_DIM
GL_W = 3 * NSA_HEADS
N_QD = DIFF_KV_HEADS * DIFF_GROUP * 2
N_KD = DIFF_KV_HEADS * 2
QD_W = N_QD * HEAD_DIM
DK_W = N_KD * HEAD_DIM
DV_W = DIFF_KV_HEADS * DIFF_VDIM

VMEM_LIMIT_BYTES = 56 * 1024 * 1024


def _cparams(n_grid_dims):
    return pltpu.CompilerParams(
        dimension_semantics=("arbitrary",) * n_grid_dims,
        vmem_limit_bytes=VMEM_LIMIT_BYTES,
    )


def _dot(a, b):
    return jnp.dot(a, b, preferred_element_type=F32)


def _dot_nt(a, b):
    return lax.dot_general(a, b, (((1,), (1,)), ((), ())), preferred_element_type=F32)


def _split_dot(x, mat):
    hi = x.astype(BF16)
    lo = (x - hi.astype(F32)).astype(BF16)
    return _dot(hi, mat) + _dot(lo, mat)


def _group_sumsq(x, gmat):
    return _split_dot(x * x, gmat)


def _block_sumsq(x, gmat):
    w = x.shape[1]
    if w <= gmat.shape[0]:
        return _group_sumsq(x, gmat[0:w, 0:w])
    cw = gmat.shape[0]
    return jnp.concatenate([_group_sumsq(x[:, c:c + cw], gmat) for c in range(0, w, cw)], axis=1)


def _lane_iota(shape):
    return lax.broadcasted_iota(jnp.int32, shape, len(shape) - 1)


def _row_iota(shape):
    return lax.broadcasted_iota(jnp.int32, shape, 0)


def _alibi_slopes(n):
    return [2.0 ** (-8.0 * (i + 1) / n) for i in range(n)]


def _row_const(rows_per_group, values):
    n = len(values)
    row = _row_iota((n * rows_per_group, 1))
    out = jnp.full((n * rows_per_group, 1), values[0], F32)
    for g in range(1, n):
        out = jnp.where(row >= g * rows_per_group, values[g], out)
    return out


def _pos_lanes(pos, width):
    lane = _lane_iota((pos.shape[0], width)) & (LANES - 1)
    hi = (pos >> 8).astype(F32)
    lo = (pos & (POS_SPLIT - 1)).astype(F32)
    return jnp.where(lane == HEAD_DIM, hi, jnp.where(lane == HEAD_DIM + 1, lo, 0.0))


def _online_step(s, v, m, l, acc):
    m_new = jnp.maximum(m, jnp.max(s, axis=-1, keepdims=True))
    alpha = jnp.exp(m - m_new)
    p = jnp.exp(s - m_new)
    l = alpha * l + jnp.sum(p, axis=-1, keepdims=True)
    acc = alpha * acc + _dot(p.astype(BF16), v)
    return m_new, l, acc


def _flash_init(m_s, l_s, acc_s):
    m_s[...] = jnp.full(m_s.shape, NEG_INF, F32)
    l_s[...] = jnp.zeros(l_s.shape, F32)
    acc_s[...] = jnp.zeros(acc_s.shape, F32)


def _flash_update(s, v, m_s, l_s, acc_s, r0):
    nrows, tk = s.shape
    sl = slice(r0, r0 + nrows)
    cols = [s[:, c:c + LANES] for c in range(0, tk, LANES)]
    smax = cols[0]
    for c in cols[1:]:
        smax = jnp.maximum(smax, c)
    m_old = m_s[sl]
    m_new = jnp.maximum(m_old, jnp.max(smax, axis=-1, keepdims=True))
    alpha = jnp.exp(m_old - m_new)
    ps = [jnp.exp(c - m_new) for c in cols]
    psum = ps[0]
    for p in ps[1:]:
        psum = psum + p
    l_s[sl] = alpha * l_s[sl] + psum
    p_bf = jnp.concatenate([p.astype(BF16) for p in ps], axis=1)
    acc_s[sl] = alpha * acc_s[sl] + _dot(p_bf, v)
    m_s[sl] = m_new


def _flash_finish(m_s, l_s, acc_s):
    l = jnp.sum(l_s[...], axis=-1, keepdims=True)
    return jnp.where(m_s[...] > 0.5 * NEG_INF, acc_s[...] / jnp.maximum(l, 1e-30), 0.0)


def _flash_loop(qk, consume, last):
    qk(0, 0)

    def body(j, carry):
        kt = 2 * j
        qk(kt + 1, 1)
        consume(kt, 0, False)
        qk(kt + 2, 0)
        consume(kt + 1, 1, False)
        return carry

    lax.fori_loop(0, last >> 1, body, 0)

    @pl.when((last & 1) == 1)
    def _():
        qk(last, 1)
        consume(last - 1, 0, False)
        consume(last, 1, True)

    @pl.when((last & 1) == 0)
    def _():
        consume(last, 0, True)


def _finish(m, l, acc):
    return jnp.where(m > 0.5 * NEG_INF, acc / jnp.maximum(l, 1e-30), 0.0)


def _compress(x, wcol):
    r = x.shape[0]
    return jnp.sum((x * wcol).reshape(r // CMP_BLOCK, CMP_BLOCK, x.shape[1]), axis=1)


def _norm_compressed(raw, kcg):
    is_k = _lane_iota(raw.shape) < HEAD_DIM
    ss = jnp.sum(jnp.where(is_k, raw * raw, 0.0), axis=-1, keepdims=True)
    return jnp.where(is_k, raw * lax.rsqrt(ss * (1.0 / HEAD_DIM) + EPS) * kcg, raw)


def _ada_kernel(c_ref, w_ref, b_ref, o_ref):
    c = c_ref[...]
    sc = c * jax.nn.sigmoid(c)
    o_ref[...] = _dot(sc.astype(BF16), w_ref[...].astype(BF16)) + b_ref[...]


def _ada(c, w_ada, b_ada):
    m, d = c.shape
    n = w_ada.shape[1]
    tn = d
    return pl.pallas_call(
        _ada_kernel,
        grid=(n // tn,),
        in_specs=[
            pl.BlockSpec((m, d), lambda j: (0, 0)),
            pl.BlockSpec((d, tn), lambda j: (0, j)),
            pl.BlockSpec((1, tn), lambda j: (0, j)),
        ],
        out_specs=pl.BlockSpec((m, tn), lambda j: (0, j)),
        out_shape=jax.ShapeDtypeStruct((m, n), F32),
        compiler_params=_cparams(1),
        name="ada",
    )(c, w_ada, b_ada.reshape(1, n))


_C_QN = 0
_C_QD = _C_QN + NSA_HEADS * LANES
_C_KSEL = _C_QD + N_QD * LANES
_C_KWIN = _C_KSEL + LANES
_C_KD = _C_KWIN + LANES
_C_KV = _C_KD + N_KD * LANES
_C_WIN = _C_KV + KV_W
_C_MG = _C_WIN + WIN_W


def _inproj_kernel(*refs, compress, seq, steps_per_seq):
    (x_ref, sh_ref, sc_ref, g1_ref, w_ref, gmat_ref, qg_ref, qaug_ref, dqg_ref, dqaug_ref, selg_ref,
     wing_ref, dkg_ref) = refs[:13]
    n_in = 13
    if compress:
        wcol_ref, pe_ref, kcg_ref = refs[13:16]
        n_in = 16
    (kv_ref, win_ref, qn_ref, qd_ref, ksel_ref, kwin_ref, vnsa_ref, kd_ref, vd_ref, gl_ref,
     gm_ref) = refs[n_in:n_in + 11]
    tm, d_model = x_ref.shape
    c_gl = _C_MG + 2 * d_model

    x = x_ref[...]
    ms = jnp.mean(x * x, axis=-1, keepdims=True)
    h = x * lax.rsqrt(ms + EPS) * g1_ref[...]
    h = h * (1.0 + sc_ref[0]) + sh_ref[0]
    hb = h.astype(BF16)

    def proj(a, b):
        return _dot(hb, w_ref[:, a:b])

    def normed(z, gain):
        return z * lax.rsqrt(_block_sumsq(z, gmat_ref[...]) + EPS) * gain

    qn_ref[...] = (normed(proj(_C_QN, _C_QD), qg_ref[...]) + qaug_ref[...]).astype(BF16)
    qd_ref[...] = (normed(proj(_C_QD, _C_KSEL), dqg_ref[...]) + dqaug_ref[...]).astype(BF16)

    pos = (pl.program_id(0) % steps_per_seq) * tm + _row_iota((tm, 1))
    del seq
    ksel = normed(proj(_C_KSEL, _C_KWIN), selg_ref[...]) + _pos_lanes(pos, LANES)
    block_onehot = jnp.where(_lane_iota((tm, LANES)) == (pos >> 6), 1.0, 0.0)
    ksel_ref[...] = jnp.concatenate([ksel, block_onehot], axis=1).astype(BF16)
    kwin_ref[...] = (normed(proj(_C_KWIN, _C_KD), wing_ref[...]) + _pos_lanes(pos, LANES)).astype(BF16)
    kd_ref[...] = (normed(proj(_C_KD, _C_KV), dkg_ref[...])
                   + _pos_lanes(pos, N_KD * LANES)).astype(BF16)

    kv = proj(_C_KV, _C_WIN)
    kv_ref[...] = kv
    vd_ref[...] = kv[:, 4 * HEAD_DIM + DK_W:].astype(BF16)
    win = proj(_C_WIN, _C_MG)
    win_ref[...] = win
    sel_v_first = pltpu.roll(kv[:, 2 * HEAD_DIM:4 * HEAD_DIM], HEAD_DIM, axis=1)
    vnsa_ref[...] = jnp.where(_lane_iota((tm, LANES)) < HEAD_DIM, sel_v_first, win).astype(BF16)

    gm_ref[...] = jax.nn.sigmoid(proj(_C_MG, c_gl))
    gl_ref[...] = jax.nn.sigmoid(proj(c_gl, c_gl + LANES))

    if compress:
        kcvc_ref = refs[n_in + 11]
        raw = _compress(kv[:, 0:2 * HEAD_DIM] + pe_ref[...], wcol_ref[...])
        kcvc_ref[...] = _norm_compressed(raw, kcg_ref[...])


def _inproj(x2d, sh, sc, tm, p, seq, compress):
    n, d = x2d.shape
    steps = n // tm
    r = sh.shape[1]
    steps_per_group = steps // sh.shape[0]
    w = p["w_in"]

    def const(a):
        return pl.BlockSpec(a.shape, lambda i: (0,) * a.ndim, pipeline_mode=pl.Buffered(1))

    def rows(width):
        return pl.BlockSpec((tm, width), lambda i: (i, 0))

    mod_spec = pl.BlockSpec((1, r, d), lambda i: (i // steps_per_group, 0, 0))
    consts = [p["g1"], w, p["gmat128"], p["qg"], p["qaug"], p["dqg"], p["dqaug"], p["selg"],
              p["wing"], p["dkg"]]
    ins = [x2d, sh, sc] + consts
    in_specs = [rows(d), mod_spec, mod_spec] + [const(a) for a in consts]
    outs = [(KV_W, F32), (WIN_W, F32), (NSA_HEADS * LANES, BF16), (N_QD * LANES, BF16),
            (2 * LANES, BF16), (LANES, BF16), (LANES, BF16), (N_KD * LANES, BF16), (DV_W, BF16),
            (LANES, F32), (2 * d, F32)]
    out_shape = [jax.ShapeDtypeStruct((n, wd), dt) for wd, dt in outs]
    out_specs = [rows(wd) for wd, _ in outs]
    if compress:
        cins = [p["wcol"], p["pe"], p["kcg"]]
        ins += cins
        in_specs += [const(a) for a in cins]
        out_shape.append(jax.ShapeDtypeStruct((n // CMP_BLOCK, LANES), F32))
        out_specs.append(pl.BlockSpec((tm // CMP_BLOCK, LANES), lambda i: (i, 0)))
    return pl.pallas_call(
        functools.partial(_inproj_kernel, compress=compress, seq=seq,
                          steps_per_seq=max(seq // tm, 1)),
        grid=(steps,),
        in_specs=in_specs,
        out_specs=out_specs,
        out_shape=out_shape,
        compiler_params=_cparams(1),
        name="inproj_cmp" if compress else "inproj",
    )(*ins)


def _stack_blocks(q, n):
    return jnp.concatenate([q[:, h * LANES:(h + 1) * LANES] for h in range(n)], axis=0)


def _pair_sum(x, n_out):
    n2 = x.shape[1]
    r = _row_iota((n2, n_out))
    c = _lane_iota((n2, n_out))
    pmat = jnp.where((r >> 1) == c, 1.0, 0.0).astype(BF16)
    hi = x.astype(BF16)
    r1 = x - hi.astype(F32)
    mid = r1.astype(BF16)
    lo = (r1 - mid.astype(F32)).astype(BF16)
    return _dot(hi, pmat) + _dot(mid, pmat) + _dot(lo, pmat)


def _top_blocks(score, n_top):
    idx = _lane_iota(score.shape).astype(F32)
    big = float(score.shape[1])
    work, sel = score, jnp.zeros_like(score)
    for _ in range(n_top):
        mx = jnp.max(work, axis=-1, keepdims=True)
        first = jnp.min(jnp.where(work == mx, idx, big), axis=-1, keepdims=True)
        hit = idx == first
        work, sel = jnp.where(hit, -2.0, work), jnp.where(hit, 1.0, sel)
    return sel


def _rank_select(score_row, n_top):
    n = score_row.shape[1]
    r = jnp.broadcast_to(score_row, (n, n))
    c = r.T
    i = _row_iota((n, n))
    j = _lane_iota((n, n))
    beats = (c > r) | ((c == r) & (i < j))
    rank = jnp.sum(jnp.where(beats, 1.0, 0.0), axis=0, keepdims=True)
    return rank < n_top


def _expand_blocks(sel_bf, first_block, n_keys):
    nb = sel_bf.shape[1]
    b = _row_iota((nb, n_keys))
    j = _lane_iota((nb, n_keys))
    emat = jnp.where(b == first_block + (j >> 6), 1.0, 0.0).astype(BF16)
    return _dot(sel_bf, emat)


_ROW_CHUNK = 256


def _nsa_kernel(qn_ref, kcvc_ref, ksel_ref, kwin_ref, v_ref, gate_ref, o_ref, m_s, l_s, acc_s,
                s_buf, *, tq, tk, n_sel):
    qi = pl.program_id(1)
    nh = NSA_HEADS
    rows = nh * tq
    q0 = qi * tq
    qs = _stack_blocks(qn_ref[...], nh)
    qpos_t = q0 + _row_iota((tq, 1))

    def head(a, h):
        return a[h * tq:(h + 1) * tq]

    kcvc = kcvc_ref[...]
    n_cmp = kcvc.shape[0]
    cmp_end = _row_iota((n_cmp, 1)) * CMP_BLOCK + (CMP_BLOCK - 1)
    kc = jnp.where(_lane_iota(kcvc.shape) < HEAD_DIM, kcvc, _pos_lanes(cmp_end, LANES)).astype(BF16)
    ok_c = (_lane_iota((1, n_cmp)) * CMP_BLOCK + (CMP_BLOCK - 1)) <= qpos_t
    s_all = _dot_nt(qs, kc)
    p_heads = jnp.zeros((tq, n_cmp), F32)
    p_bf = []
    for h in range(nh):
        s = jnp.where(ok_c, head(s_all, h), NEG_INF)
        e = jnp.where(ok_c, jnp.exp(s - jnp.max(s, axis=-1, keepdims=True)), 0.0)
        p = e / jnp.maximum(jnp.sum(e, axis=-1, keepdims=True), 1e-30)
        p_heads = p_heads + p
        p_bf.append(p.astype(BF16))
    o_c = _dot(jnp.concatenate(p_bf, axis=0), kcvc.astype(BF16))
    imp = _pair_sum(p_heads, LANES)
    blk = _lane_iota((1, LANES))
    forced = (blk == (qpos_t >> 6)) | (blk == 0)
    valid = blk * SEL_BLOCK <= qpos_t
    score = jnp.where(valid, jnp.where(forced, FORCE_SCORE, imp), -1.0)

    span = tq + WINDOW
    start = pl.multiple_of(jnp.maximum(q0 - WINDOW, 0), tq)
    kw = kwin_ref[pl.ds(start, span), :]
    vw = v_ref[pl.ds(start, span), :]
    dist = qpos_t - (start + _lane_iota((1, span)))
    ok_w = (dist >= 0) & (dist < WINDOW)
    s_all = _dot_nt(qs, kw)
    p_bf, l_w = [], []
    for h in range(nh):
        s = jnp.where(ok_w, head(s_all, h), NEG_INF)
        e = jnp.exp(s - jnp.max(s, axis=-1, keepdims=True))
        l_w.append(jnp.sum(e, axis=-1, keepdims=True))
        p_bf.append(e.astype(BF16))
    o_w = _dot(jnp.concatenate(p_bf, axis=0), vw)

    sel = _top_blocks(score, min(TOP_N, n_sel))
    desel = jnp.where(valid & (sel > 0.5), 0.0, MASK_BIAS).astype(BF16)
    qfull = jnp.concatenate([qs, jnp.concatenate([desel] * nh, axis=0)], axis=1)

    _flash_init(m_s, l_s, acc_s)
    rc = min(_ROW_CHUNK, rows)

    def sel_qk(kt, slot):
        k = ksel_ref[pl.ds(pl.multiple_of(kt * tk, tk), tk), :]
        s_buf[slot] = _dot_nt(qfull, k)

    def sel_consume(kt, slot, causal):
        v = v_ref[pl.ds(pl.multiple_of(kt * tk, tk), tk), :]
        for r0 in range(0, rows, rc):
            s = s_buf[slot, r0:r0 + rc, :]
            if causal:
                qp = q0 + ((r0 + _row_iota((rc, 1))) & (tq - 1))
                s = jnp.where(kt * tk + _lane_iota((1, tk)) <= qp, s, NEG_INF)
            _flash_update(s, v, m_s, l_s, acc_s, r0)

    _flash_loop(sel_qk, sel_consume, q0 // tk)
    o_s = _flash_finish(m_s, l_s, acc_s)

    gate = gate_ref[...]
    low = _lane_iota((tq, LANES)) < HEAD_DIM
    outs = []
    for h in range(nh):
        upper = gate[:, 3 * h:3 * h + 1] * head(o_c, h) + gate[:, 3 * h + 2:3 * h + 3] * (head(o_w, h) / l_w[h])
        both = jnp.where(low, gate[:, 3 * h + 1:3 * h + 2] * head(o_s, h), upper)
        outs.append(both + pltpu.roll(both, HEAD_DIM, axis=1))
    for j in range(nh // 2):
        o_ref[:, j * LANES:(j + 1) * LANES] = jnp.where(low, outs[2 * j], outs[2 * j + 1]).astype(o_ref.dtype)


def _nsa_prompt(qn, kcvc, ksel, kwin, vnsa, gates, batch, seq, tq, tk):
    nq = seq // tq
    n_cmp = seq // CMP_BLOCK
    n_sel = seq // SEL_BLOCK
    assert n_sel <= LANES and seq % tk == 0 and tk % tq == 0 and seq >= tq + WINDOW
    rows = NSA_HEADS * tq

    def qrow(width):
        return pl.BlockSpec((tq, width), lambda b, i: (b * nq + i, 0))

    def per_batch(rows_, width):
        return pl.BlockSpec((rows_, width), lambda b, i: (b, 0))

    return pl.pallas_call(
        functools.partial(_nsa_kernel, tq=tq, tk=tk, n_sel=n_sel),
        grid=(batch, nq),
        in_specs=[qrow(NSA_HEADS * LANES), per_batch(n_cmp, LANES), per_batch(seq, 2 * LANES),
                  per_batch(seq, LANES), per_batch(seq, LANES), qrow(LANES)],
        out_specs=qrow(NSA_W),
        out_shape=jax.ShapeDtypeStruct((batch * seq, NSA_W), BF16),
        scratch_shapes=[pltpu.VMEM((rows, LANES), F32), pltpu.VMEM((rows, LANES), F32),
                        pltpu.VMEM((rows, LANES), F32), pltpu.VMEM((2, rows, tk), F32)],
        compiler_params=_cparams(2),
        name="nsa_prompt",
    )(qn, kcvc, ksel, kwin, vnsa, gates)


def _lambda_full(lam_ref, lam_init):
    dl = lam_ref[...]
    a = jnp.sum(dl[0:1] * dl[1:2], axis=-1, keepdims=True)
    b = jnp.sum(dl[2:3] * dl[3:4], axis=-1, keepdims=True)
    return jnp.exp(a) - jnp.exp(b) + lam_init


def _subln(o, gain, lam_init):
    ms = jnp.mean(o * o, axis=-1, keepdims=True)
    return o * lax.rsqrt(ms + EPS) * gain * (1.0 - lam_init)


def _diff_kernel(q_ref, k_ref, v_ref, lam_ref, subln_ref, o_ref, m_s, l_s, acc_s, s_buf, *, tq, tk,
                 lam_init):
    qi = pl.program_id(2)
    g = DIFF_GROUP
    half = g * tq
    rows = 2 * half
    q0 = qi * tq
    q = q_ref[...]
    qm = [jnp.concatenate([q[:, (gg * 2 + m) * LANES:(gg * 2 + m + 1) * LANES] for gg in range(g)],
                          axis=0) for m in range(2)]
    _flash_init(m_s, l_s, acc_s)
    rc = min(_ROW_CHUNK, half)

    def qk(kt, slot):
        k = k_ref[pl.ds(pl.multiple_of(kt * tk, tk), tk), :]
        for m in range(2):
            s_buf[slot, m * half:(m + 1) * half, :] = _dot_nt(qm[m], k[:, m * LANES:(m + 1) * LANES])

    def consume(kt, slot, causal):
        v = v_ref[pl.ds(pl.multiple_of(kt * tk, tk), tk), :]
        for r0 in range(0, rows, rc):
            s = s_buf[slot, r0:r0 + rc, :]
            if causal:
                qp = q0 + ((r0 + _row_iota((rc, 1))) & (tq - 1))
                s = jnp.where(kt * tk + _lane_iota((1, tk)) <= qp, s, NEG_INF)
            _flash_update(s, v, m_s, l_s, acc_s, r0)

    _flash_loop(qk, consume, q0 // tk)
    o = _flash_finish(m_s, l_s, acc_s)
    lam = _lambda_full(lam_ref, lam_init)
    for gg in range(g):
        r0 = gg * tq
        d = o[r0:r0 + tq] - lam * o[half + r0:half + r0 + tq]
        o_ref[:, gg * DIFF_VDIM:(gg + 1) * DIFF_VDIM] = _subln(d, subln_ref[...], lam_init).astype(o_ref.dtype)


def _diff_prompt(qd, kd, vd, lam, subln, batch, seq, tq, tk, lam_init):
    nq = seq // tq
    assert seq % tk == 0 and tk % tq == 0
    rows = 2 * DIFF_GROUP * tq
    return pl.pallas_call(
        functools.partial(_diff_kernel, tq=tq, tk=tk, lam_init=lam_init),
        grid=(batch, DIFF_KV_HEADS, nq),
        in_specs=[
            pl.BlockSpec((tq, DIFF_GROUP * 2 * LANES), lambda b, n, i: (b * nq + i, n)),
            pl.BlockSpec((seq, 2 * LANES), lambda b, n, i: (b, n)),
            pl.BlockSpec((seq, DIFF_VDIM), lambda b, n, i: (b, n)),
            pl.BlockSpec(lam.shape, lambda b, n, i: (0, 0)),
            pl.BlockSpec(subln.shape, lambda b, n, i: (0, 0)),
        ],
        out_specs=pl.BlockSpec((tq, DIFF_GROUP * DIFF_VDIM), lambda b, n, i: (b * nq + i, n)),
        out_shape=jax.ShapeDtypeStruct((batch * seq, QD_W), BF16),
        scratch_shapes=[pltpu.VMEM((rows, LANES), F32), pltpu.VMEM((rows, LANES), F32),
                        pltpu.VMEM((rows, DIFF_VDIM), F32), pltpu.VMEM((2, rows, tk), F32)],
        compiler_params=_cparams(3),
        name="diff_prompt",
    )(qd, kd, vd, lam, subln)


_QROWS = 16
_PAGES_PER_STEP = 16


def _decode_kernel(pt_ref, *refs, n_chunks, lam_init):
    del pt_ref
    npg = _PAGES_PER_STEP
    pages = refs[:npg]
    (qn_ref, qd_ref, selkv_new_ref, winkv_new_ref, kd_new_ref, vd_new_ref, gate_ref, win_ref,
     gmat_ref, wcol_ref, pe_ref, kcg_ref, selg_ref, wing_ref, dkg_ref, lam_ref,
     subln_ref) = refs[npg:npg + 17]
    oa_ref, ob_ref = refs[npg + 17:npg + 19]
    (cmp_buf, kd_a, vd_a, kd_b, vd_b, stash, sel_buf, kcvc_s, dm_s, dl_s, dacc_s) = refs[npg + 19:]
    c = pl.program_id(1)
    rows_per_step = npg * PAGE_SIZE
    past_len = n_chunks * rows_per_step
    nh = NSA_HEADS

    slopes = _alibi_slopes(DIFF_KV_HEADS * DIFF_GROUP)
    dslope = _row_const(1, [slopes[r // 2] for r in range(8)] + [0.0] * (_QROWS - 8))
    qd = qd_ref[...]
    qd_gained = (qd.astype(F32) * dkg_ref[...]).astype(BF16)

    def normalise(kd_buf, vd_buf):
        for i in range(npg):
            pg = pages[i]
            r0 = i * PAGE_SIZE
            cmp_buf[r0:r0 + PAGE_SIZE, :] = pg[:, 0:2 * HEAD_DIM]
            sk = pg[:, 2 * HEAD_DIM:4 * HEAD_DIM]
            ms = _dot((sk * sk).astype(BF16), gmat_ref[0:LANES, 0:LANES])
            r = jnp.where(_lane_iota(sk.shape) < HEAD_DIM, lax.rsqrt(ms + EPS), 1.0)
            base = pl.multiple_of(c * rows_per_step + r0, PAGE_SIZE)
            stash[pl.ds(base, PAGE_SIZE), :] = (sk * r).astype(BF16)
            dk = pg[:, 4 * HEAD_DIM:4 * HEAD_DIM + DK_W]
            ms = _dot((dk * dk).astype(BF16), gmat_ref[...])
            kd_buf[r0:r0 + PAGE_SIZE, :] = (dk * lax.rsqrt(ms + EPS)).astype(BF16)
            vd_buf[r0:r0 + PAGE_SIZE, :] = pg[:, 4 * HEAD_DIM + DK_W:].astype(BF16)
        raw = _compress(cmp_buf[...] + pe_ref[...], wcol_ref[...])
        cmp_rows = rows_per_step // CMP_BLOCK
        kcvc_s[pl.ds(pl.multiple_of(c * cmp_rows, cmp_rows), cmp_rows), :] = _norm_compressed(
            raw, kcg_ref[...])

    def attend(kd_buf, vd_buf, chunk, live):
        kpos = chunk * rows_per_step + _lane_iota((1, rows_per_step))
        sc = _dot_nt(qd_gained, kd_buf[...]) - dslope * (past_len - kpos).astype(F32)
        m_old = dm_s[...]
        m_new = jnp.where(live, jnp.maximum(m_old, jnp.max(sc, axis=-1, keepdims=True)), m_old)
        alpha = jnp.exp(m_old - m_new)
        p = jnp.where(live, jnp.exp(sc - m_new), 0.0)
        dl_s[...] = alpha * dl_s[...] + jnp.sum(p, axis=-1, keepdims=True)
        dacc_s[...] = alpha * dacc_s[...] + _dot(p.astype(BF16), vd_buf[...])
        dm_s[...] = m_new

    @pl.when(c == 0)
    def _():
        dm_s[...] = jnp.full(dm_s.shape, NEG_INF, F32)
        dl_s[...] = jnp.zeros(dl_s.shape, F32)
        dacc_s[...] = jnp.zeros(dacc_s.shape, F32)
        kd_b[...] = jnp.zeros(kd_b.shape, BF16)
        vd_b[...] = jnp.zeros(vd_b.shape, BF16)

    @pl.when((c & 1) == 0)
    def _():
        attend(kd_b, vd_b, c - 1, c > 0)
        normalise(kd_a, vd_a)

    @pl.when((c & 1) == 1)
    def _():
        attend(kd_a, vd_a, c - 1, True)
        normalise(kd_b, vd_b)

    @pl.when(c == n_chunks - 1)
    def _():
        if (n_chunks - 1) % 2 == 0:
            attend(kd_a, vd_a, c, True)
        else:
            attend(kd_b, vd_b, c, True)
        dm, dl, dacc = dm_s[...], dl_s[...], dacc_s[...]
        kd_new = kd_new_ref[...].astype(F32)
        s_new = jnp.sum(qd.astype(F32) * kd_new, axis=-1, keepdims=True)
        m2 = jnp.maximum(dm, s_new)
        alpha = jnp.exp(dm - m2)
        p_new = jnp.exp(s_new - m2)
        l2 = alpha * dl + p_new
        acc2 = alpha * dacc + p_new * vd_new_ref[...].astype(F32)
        od = acc2 / l2
        lam = _lambda_full(lam_ref, lam_init)
        for n in range(DIFF_KV_HEADS):
            for g in range(DIFF_GROUP):
                r = (n * DIFF_GROUP + g) * 2
                lo, hi = n * DIFF_VDIM, (n + 1) * DIFF_VDIM
                d = od[r:r + 1, lo:hi] - lam * od[r + 1:r + 2, lo:hi]
                col = (n * DIFF_GROUP + g) * DIFF_VDIM
                ob_ref[:, col:col + DIFF_VDIM] = _subln(d, subln_ref[...], lam_init).astype(ob_ref.dtype)

        qn = qn_ref[...]
        aslope = _row_const(1, _alibi_slopes(nh) + [0.0] * (_QROWS - nh))
        kcvc = kcvc_s[...]
        n_cmp = kcvc.shape[0]
        cmp_end = _lane_iota((1, n_cmp)) * CMP_BLOCK + (CMP_BLOCK - 1)
        s = _dot_nt(qn, kcvc[:, :HEAD_DIM].astype(BF16)) - aslope * (past_len - cmp_end).astype(F32)
        e = jnp.exp(s - jnp.max(s, axis=-1, keepdims=True))
        p_c = e / jnp.sum(e, axis=-1, keepdims=True)
        o_c = _dot(p_c.astype(BF16), kcvc[:, HEAD_DIM:].astype(BF16))
        head_row = _row_iota((_QROWS, 1)) < nh
        p_sum = jnp.sum(jnp.where(head_row, p_c, 0.0), axis=0, keepdims=True)
        n_blk = n_cmp // 2
        imp = _pair_sum(jnp.broadcast_to(p_sum, (8, n_cmp)), LANES)[0:1]
        blk = _lane_iota((1, LANES))
        score = jnp.where(blk == 0, FORCE_SCORE, jnp.where(blk < n_blk, imp, -1.0))
        n_pick = min(TOP_N - 1, n_blk)
        sel = _rank_select(score, n_pick)
        r = _row_iota((LANES, LANES))
        upper = jnp.where(r < _lane_iota((LANES, LANES)), 1.0, 0.0).astype(BF16)
        sel8 = jnp.broadcast_to(jnp.where(sel, 1.0, 0.0), (8, LANES)).astype(BF16)
        slot_of = _dot(sel8, upper)[0:1]
        slot = _row_iota((_QROWS, 1)).astype(F32)
        block_of = jnp.sum(jnp.where(sel & (slot_of == slot), blk.astype(F32), 0.0),
                           axis=-1, keepdims=True).astype(jnp.int32)
        lane_slot = _lane_iota((1, _QROWS * SEL_BLOCK)) >> 6
        kpos = _lane_iota((1, _QROWS * SEL_BLOCK)) & (SEL_BLOCK - 1)
        for t in range(n_pick):
            b_t = block_of[t, 0]
            src = pl.multiple_of(b_t * SEL_BLOCK, SEL_BLOCK)
            sel_buf[t * SEL_BLOCK:(t + 1) * SEL_BLOCK, :] = stash[pl.ds(src, SEL_BLOCK), :]
            kpos = jnp.where(lane_slot == t, kpos + b_t * SEL_BLOCK, kpos)
        sel_buf[n_pick * SEL_BLOCK:, :] = jnp.zeros(((_QROWS - n_pick) * SEL_BLOCK, LANES), BF16)
        kv = sel_buf[...]
        qn_gained = (qn.astype(F32) * selg_ref[:, 0:HEAD_DIM]).astype(BF16)
        sc = _dot_nt(qn_gained, kv[:, :HEAD_DIM]) - aslope * (past_len - kpos).astype(F32)
        sc = jnp.where(lane_slot < n_pick, sc, NEG_INF)

        init = (jnp.full((_QROWS, 1), NEG_INF, F32), jnp.zeros((_QROWS, 1), F32),
                jnp.zeros((_QROWS, HEAD_DIM), F32))
        sm, sl_, sacc = _online_step(sc, kv[:, HEAD_DIM:], *init)
        qf = qn.astype(F32)

        def add_new(m, l, acc, kv_new):
            s_new = jnp.sum(qf * kv_new[:, :HEAD_DIM], axis=-1, keepdims=True)
            m2 = jnp.maximum(m, s_new)
            alpha = jnp.exp(m - m2)
            p_new = jnp.exp(s_new - m2)
            return (alpha * acc + p_new * kv_new[:, HEAD_DIM:]) / (alpha * l + p_new)

        o_s = add_new(sm, sl_, sacc, selkv_new_ref[...].astype(F32))

        win = win_ref[...]
        w_buf = win.shape[0]
        ms = _group_sumsq(win, gmat_ref[0:LANES, 0:LANES])
        r = jnp.where(_lane_iota(win.shape) < HEAD_DIM, lax.rsqrt(ms + EPS), 1.0)
        wkv = (win * r * wing_ref[...]).astype(BF16)
        dist_w = w_buf - _lane_iota((1, w_buf))
        sc = _dot_nt(qn, wkv[:, :HEAD_DIM]) - aslope * dist_w.astype(F32)
        sc = jnp.where(dist_w < WINDOW, sc, NEG_INF)
        o_w = add_new(*_online_step(sc, wkv[:, HEAD_DIM:], *init), winkv_new_ref[...].astype(F32))

        gate = gate_ref[...]
        for h in range(nh):
            o = (gate[:, 3 * h:3 * h + 1] * o_c[h:h + 1] + gate[:, 3 * h + 1:3 * h + 2] * o_s[h:h + 1]
                 + gate[:, 3 * h + 2:3 * h + 3] * o_w[h:h + 1])
            oa_ref[:, h * HEAD_DIM:(h + 1) * HEAD_DIM] = o.astype(oa_ref.dtype)


def _decode(l, page_table, cache, qn16, qd16, selkv_new, winkv_new, kd_new, vd_new, gates,
            win_state, p, lam_init):
    nb, n_pages = page_table.shape
    npg = _PAGES_PER_STEP
    n_chunks = n_pages // npg
    rows_per_step = npg * PAGE_SIZE
    past_len = n_pages * PAGE_SIZE

    def page_spec(i):
        return pl.BlockSpec((None, None, PAGE_SIZE, KV_W),
                            lambda b, c, pt: (l, pt[b, c * npg + i], 0, 0))

    def per_seq(a):
        if a.ndim == 4:
            return pl.BlockSpec((None, None) + a.shape[2:], lambda b, c, pt: (l, b, 0, 0))
        return pl.BlockSpec((None,) + a.shape[1:], lambda b, c, pt: (b,) + (0,) * (a.ndim - 1))

    def const(a):
        return pl.BlockSpec(a.shape, lambda b, c, pt: (0,) * a.ndim)

    seq_ins = [qn16, qd16, selkv_new, winkv_new, kd_new, vd_new, gates, win_state]
    const_ins = [p["gmat64"], p["wcol_dec"], p["pe_dec"], p["kcg"], p["selg64"], p["wing64"],
                 p["dkg64"], p["lam"], p["subln"]]
    grid_spec = pltpu.PrefetchScalarGridSpec(
        num_scalar_prefetch=1,
        grid=(nb, n_chunks),
        in_specs=[page_spec(i) for i in range(npg)] + [per_seq(a) for a in seq_ins]
        + [const(a) for a in const_ins],
        out_specs=[pl.BlockSpec((None, 1, NSA_W), lambda b, c, pt: (b, 0, 0)),
                   pl.BlockSpec((None, 1, QD_W), lambda b, c, pt: (b, 0, 0))],
        scratch_shapes=[
            pltpu.VMEM((rows_per_step, LANES), F32),
            pltpu.VMEM((rows_per_step, DK_W), BF16),
            pltpu.VMEM((rows_per_step, DV_W), BF16),
            pltpu.VMEM((rows_per_step, DK_W), BF16),
            pltpu.VMEM((rows_per_step, DV_W), BF16),
            pltpu.VMEM((past_len, LANES), BF16),
            pltpu.VMEM((_QROWS * SEL_BLOCK, LANES), BF16),
            pltpu.VMEM((past_len // CMP_BLOCK, LANES), F32),
            pltpu.VMEM((_QROWS, 1), F32),
            pltpu.VMEM((_QROWS, 1), F32),
            pltpu.VMEM((_QROWS, DV_W), F32),
        ],
    )
    return pl.pallas_call(
        functools.partial(_decode_kernel, n_chunks=n_chunks, lam_init=lam_init),
        grid_spec=grid_spec,
        out_shape=[jax.ShapeDtypeStruct((nb, 1, NSA_W), BF16),
                   jax.ShapeDtypeStruct((nb, 1, QD_W), BF16)],
        compiler_params=_cparams(2),
        name="decode",
    )(page_table, *([cache] * npg), *seq_ins, *const_ins)


def _merge_kernel(oa_ref, ob_ref, gm_ref, x_ref, ga_ref, wa_ref, wb_ref, wo_ref, o_ref):
    d = x_ref.shape[-1]
    ya = _dot(oa_ref[...], wa_ref[...])
    yb = _dot(ob_ref[...], wb_ref[...])
    mg = gm_ref[:, 0:d] * ya + gm_ref[:, d:2 * d] * yb
    o_ref[...] = x_ref[...] + ga_ref[0] * _dot(mg.astype(BF16), wo_ref[...])


def _merge(oa, ob, gm, x2d, ga, tm, p):
    n, d = x2d.shape
    steps = n // tm
    r = ga.shape[1]
    steps_per_group = steps // ga.shape[0]

    def rows(width):
        return pl.BlockSpec((tm, width), lambda i: (i, 0))

    def const(a):
        return pl.BlockSpec(a.shape, lambda i: (0,) * a.ndim)

    return pl.pallas_call(
        _merge_kernel,
        grid=(steps,),
        in_specs=[rows(NSA_W), rows(QD_W), rows(2 * d), rows(d),
                  pl.BlockSpec((1, r, d), lambda i: (i // steps_per_group, 0, 0)),
                  const(p["w_br_nsa"]), const(p["w_br_diff"]), const(p["w_out"])],
        out_specs=rows(d),
        out_shape=jax.ShapeDtypeStruct((n, d), F32),
        compiler_params=_cparams(1),
        name="merge",
    )(oa, ob, gm, x2d, ga, p["w_br_nsa"], p["w_br_diff"], p["w_out"])


def _ffn_kernel(x_ref, sh_ref, sc_ref, ga_ref, g2_ref, wi_ref, wo_ref, o_ref):
    x = x_ref[...]
    d_ff = wo_ref.shape[0]
    ms = jnp.mean(x * x, axis=-1, keepdims=True)
    h = x * lax.rsqrt(ms + EPS) * g2_ref[...]
    hb = (h * (1.0 + sc_ref[0]) + sh_ref[0]).astype(BF16)
    u = _dot(hb, wi_ref[...])
    ug = u[:, 0:d_ff]
    act = (ug * jax.nn.sigmoid(ug) * u[:, d_ff:]).astype(BF16)
    o_ref[...] = x + ga_ref[0] * _dot(act, wo_ref[...])


def _ffn(x2d, sh, sc, ga, tm, p):
    n, d = x2d.shape
    steps = n // tm
    r = sh.shape[1]
    steps_per_group = steps // sh.shape[0]
    rows = pl.BlockSpec((tm, d), lambda i: (i, 0))
    mod_spec = pl.BlockSpec((1, r, d), lambda i: (i // steps_per_group, 0, 0))

    def const(a):
        return pl.BlockSpec(a.shape, lambda i: (0,) * a.ndim, pipeline_mode=pl.Buffered(1))

    return pl.pallas_call(
        _ffn_kernel,
        grid=(steps,),
        in_specs=[rows, mod_spec, mod_spec, mod_spec, const(p["g2"]), const(p["w_ffn_in"]),
                  const(p["w_ffn_out"])],
        out_specs=rows,
        out_shape=jax.ShapeDtypeStruct((n, d), F32),
        compiler_params=_cparams(1),
        name="ffn",
    )(x2d, sh, sc, ga, p["g2"], p["w_ffn_in"], p["w_ffn_out"])


def _pad_blocks(a, n):
    lead = a.shape[:-1]
    a = a.reshape(lead + (n, HEAD_DIM))
    a = jnp.concatenate([a, jnp.zeros_like(a)], axis=-1)
    return a.reshape(lead + (n * LANES,))


def _block_gain(g, reps, scale=1.0):
    return _pad_blocks(jnp.tile(g.astype(F32), reps) * scale, reps).reshape(1, -1)


def _slope_lanes(slopes):
    out = jnp.zeros((len(slopes), LANES), F32)
    s = jnp.asarray(slopes, F32)
    out = out.at[:, HEAD_DIM].set(s * POS_SPLIT).at[:, HEAD_DIM + 1].set(s)
    return out.reshape(1, -1)


def _half_gain(g):
    return jnp.concatenate([g.astype(F32), jnp.ones((HEAD_DIM,), F32)]).reshape(1, -1)


def _compress_weights(wk, wv, rows):
    col = jnp.concatenate([jnp.tile(wk.astype(F32)[:, None], (1, HEAD_DIM)),
                           jnp.tile(wv.astype(F32)[:, None], (1, HEAD_DIM))], axis=1)
    return jnp.tile(col, (rows // CMP_BLOCK, 1))


def _prepare_params(l, tm_prompt, w_ada, b_ada, norm1_g, norm2_g, w_in, nsa_q_norm, nsa_k_norm,
                    nsa_cmp_wk, nsa_cmp_wv, nsa_cmp_pe, diff_q_norm, diff_k_norm, diff_lambda,
                    diff_subln, w_br_nsa, w_br_diff, w_out, w_ffn_in, w_ffn_out):
    d = w_in.shape[1]
    wi = w_in[l]
    o_kv = NSA_W
    o_win = o_kv + KV_W
    o_gl = o_win + WIN_W
    o_qd = o_gl + GL_W
    o_mg = o_qd + QD_W
    w_kv = wi[:, o_kv:o_win]
    w_win = wi[:, o_win:o_gl]
    w_r = jnp.concatenate([
        _pad_blocks(wi[:, 0:NSA_W], NSA_HEADS),
        _pad_blocks(wi[:, o_qd:o_mg], N_QD),
        _pad_blocks(w_kv[:, 2 * HEAD_DIM:3 * HEAD_DIM], 1),
        _pad_blocks(w_win[:, 0:HEAD_DIM], 1),
        _pad_blocks(w_kv[:, 4 * HEAD_DIM:4 * HEAD_DIM + DK_W], N_KD),
        w_kv, w_win, wi[:, o_mg:o_mg + 2 * d], wi[:, o_gl:o_qd],
        jnp.zeros((d, LANES - GL_W), wi.dtype)], axis=1).astype(BF16)
    pe2 = jnp.concatenate([nsa_cmp_pe[l].astype(F32), jnp.zeros((CMP_BLOCK, HEAD_DIM), F32)], axis=1)
    dec_rows = _PAGES_PER_STEP * PAGE_SIZE
    nsa_slopes = _alibi_slopes(NSA_HEADS)
    diff_slopes = _alibi_slopes(DIFF_KV_HEADS * DIFF_GROUP)
    g64 = jnp.arange(2 * LANES) // HEAD_DIM
    g128 = jnp.arange(2 * LANES) // LANES
    return {
        "w_ada": w_ada[l], "b_ada": b_ada[l],
        "g1": norm1_g[l].astype(F32).reshape(1, -1), "g2": norm2_g[l].astype(F32).reshape(1, -1),
        "w_in": w_r,
        "gmat64": ((g64[:, None] == g64[None, :]) * (1.0 / HEAD_DIM)).astype(BF16),
        "gmat128": ((g128[:, None] == g128[None, :]) * (1.0 / HEAD_DIM)).astype(BF16),
        "qg": _block_gain(nsa_q_norm[l], NSA_HEADS, QK_SCALE),
        "qaug": _slope_lanes(nsa_slopes),
        "dqg": _block_gain(diff_q_norm[l], N_QD, QK_SCALE),
        "dqaug": _slope_lanes([diff_slopes[j // 2] for j in range(N_QD)]),
        "selg": _block_gain(nsa_k_norm[l, 1], 1), "wing": _block_gain(nsa_k_norm[l, 2], 1),
        "dkg": _block_gain(diff_k_norm[l], N_KD),
        "kcg": _half_gain(nsa_k_norm[l, 0]),
        "selg64": _half_gain(nsa_k_norm[l, 1]), "wing64": _half_gain(nsa_k_norm[l, 2]),
        "dkg64": jnp.tile(diff_k_norm[l].astype(F32), N_KD).reshape(1, -1),
        "wcol": _compress_weights(nsa_cmp_wk[l], nsa_cmp_wv[l], tm_prompt),
        "pe": jnp.tile(pe2, (tm_prompt // CMP_BLOCK, 1)),
        "wcol_dec": _compress_weights(nsa_cmp_wk[l], nsa_cmp_wv[l], dec_rows),
        "pe_dec": jnp.tile(pe2, (dec_rows // CMP_BLOCK, 1)),
        "lam": diff_lambda[l].astype(F32), "subln": diff_subln[l].astype(F32).reshape(1, -1),
        "w_br_nsa": w_br_nsa[l].astype(BF16), "w_br_diff": w_br_diff[l].astype(BF16),
        "w_out": w_out[l].astype(BF16),
        "w_ffn_in": w_ffn_in[l].astype(BF16), "w_ffn_out": w_ffn_out[l].astype(BF16),
    }


def _pad_rows(a, rows):
    return jnp.concatenate([a, jnp.zeros((a.shape[0], rows - a.shape[1]) + a.shape[2:], a.dtype)], axis=1)


def _unblock(a, n):
    return a.reshape(a.shape[0], n, LANES)[:, :, :HEAD_DIM]


def _decode_queries(qn, qd):
    nb = qn.shape[0]
    qn16 = _pad_rows(qn, _QROWS)
    q = qd.reshape(nb, DIFF_KV_HEADS, DIFF_GROUP, 2, HEAD_DIM)
    slot = jnp.arange(N_KD).reshape(DIFF_KV_HEADS, 1, 2)
    onehot = (slot[..., None] == jnp.arange(N_KD)).astype(q.dtype)
    qmat = q[:, :, :, :, None, :] * onehot[None, :, :, :, :, None]
    qd16 = _pad_rows(qmat.reshape(nb, N_QD, DK_W), _QROWS)
    return qn16, qd16


def _layer(l, xp, xs, cache_kv, state_win_kv, page_table, mod, p, tm, tiles):
    batch, seq, d = xp.shape
    nb = xs.shape[0]
    win_l = state_win_kv[l]
    lam_init = 0.8 - 0.6 * math.exp(-0.3 * l)
    mod_p = mod[:batch].reshape(batch, 1, 6 * d)
    mod_s = mod[batch:batch + nb].reshape(1, nb, 6 * d)

    def part(m, k):
        return m[:, :, k * d:(k + 1) * d]

    x2 = xp.reshape(batch * seq, d)
    (kv, win, qn, qd, ksel, kwin, vnsa, kd, vd, gl, gm, kcvc) = _inproj(
        x2, part(mod_p, 0), part(mod_p, 1), tm, p, seq, compress=True)
    oa = _nsa_prompt(qn, kcvc, ksel, kwin, vnsa, gl, batch, seq, tiles["nsa_tq"], tiles["nsa_tk"])
    ob = _diff_prompt(qd, kd, vd, p["lam"], p["subln"], batch, seq, tiles["diff_tq"],
                      tiles["diff_tk"], lam_init)
    x1 = _merge(oa, ob, gm, x2, part(mod_p, 2), tm, p)
    yp = _ffn(x1, part(mod_p, 3), part(mod_p, 4), part(mod_p, 5), tm, p)
    w_buf = win_l.shape[1]
    win_p = win.reshape(batch, seq, WIN_W)[:, seq - w_buf:]

    xs2 = xs.reshape(nb, d)
    (kv_s, win_s, qn_s, qd_s, ksel_s, kwin_s, vnsa_s, kd_s, vd_s, gl_s, gm_s) = _inproj(
        xs2, part(mod_s, 0), part(mod_s, 1), nb, p, 1, compress=False)
    qn16, qd16 = _decode_queries(_unblock(qn_s, NSA_HEADS), _unblock(qd_s, N_QD))
    selkv_s = jnp.concatenate([ksel_s[:, :HEAD_DIM], vnsa_s[:, :HEAD_DIM]], axis=1)
    winkv_s = jnp.concatenate([kwin_s[:, :HEAD_DIM], vnsa_s[:, HEAD_DIM:]], axis=1)
    kd_plain = _unblock(kd_s, N_KD).reshape(nb, DK_W)
    oa_s, ob_s = _decode(l, page_table, cache_kv, qn16, qd16, selkv_s[:, None], winkv_s[:, None],
                         kd_plain[:, None], vd_s[:, None], gl_s[:, None], state_win_kv, p, lam_init)
    x1s = _merge(oa_s.reshape(nb, NSA_W), ob_s.reshape(nb, QD_W), gm_s, xs2, part(mod_s, 2), nb, p)
    ys = _ffn(x1s, part(mod_s, 3), part(mod_s, 4), part(mod_s, 5), nb, p)
    win_state_s = jnp.concatenate([win_l[:, 1:], win_s[:, None]], axis=1)
    return (yp.reshape(batch, seq, d), ys.reshape(nb, 1, d), kv.reshape(batch, seq, KV_W),
            kv_s.reshape(nb, 1, KV_W), win_p, win_state_s)


def _tiles(seq):
    return {"nsa_tq": min(256, seq), "nsa_tk": min(512, seq), "diff_tq": min(256, seq),
            "diff_tk": min(512, seq)}


def kernel(x_prompt, x_sample, cache_kv, state_win_kv, page_table, c_prompt, c_sample, w_ada, b_ada,
           norm1_g, norm2_g, w_in, nsa_q_norm, nsa_k_norm, nsa_cmp_wk, nsa_cmp_wv, nsa_cmp_pe,
           diff_q_norm, diff_k_norm, diff_lambda, diff_subln, w_br_nsa, w_br_diff, w_out, w_ffn_in,
           w_ffn_out):
    depth = w_in.shape[0]
    seq = x_prompt.shape[1]
    assert x_sample.shape[1] == 1 and cache_kv.shape[2] == PAGE_SIZE
    tm = min(512, seq)
    xp, xs = x_prompt, x_sample
    kv_p, kv_s, win_p, win_s = [], [], [], []
    n_cond = c_prompt.shape[0] + c_sample.shape[0]
    c_all = jnp.concatenate([c_prompt, c_sample,
                             jnp.zeros((-n_cond % 16, c_prompt.shape[1]), c_prompt.dtype)], axis=0)
    for l in range(depth):
        p = _prepare_params(l, tm, w_ada, b_ada, norm1_g, norm2_g, w_in, nsa_q_norm, nsa_k_norm,
                            nsa_cmp_wk, nsa_cmp_wv, nsa_cmp_pe, diff_q_norm, diff_k_norm,
                            diff_lambda, diff_subln, w_br_nsa, w_br_diff, w_out, w_ffn_in, w_ffn_out)
        mod = _ada(c_all, p["w_ada"], p["b_ada"])
        xp, xs, kvp, kvs, wp, ws = _layer(l, xp, xs, cache_kv, state_win_kv, page_table, mod, p, tm,
                                          _tiles(seq))
        kv_p.append(kvp)
        kv_s.append(kvs)
        win_p.append(wp)
        win_s.append(ws)
    return (xp, xs, jnp.stack(kv_p), jnp.stack(kv_s), jnp.stack(win_p), jnp.stack(win_s))
```

```python
import functools
import math

import jax
import jax.numpy as jnp
from jax import lax
from jax.experimental import pallas as pl
from jax.experimental.pallas import tpu as pltpu

F32 = jnp.float32
BF16 = jnp.bfloat16

HEAD_DIM = 64
NSA_HEADS = 8
CMP_BLOCK = 32
SEL_BLOCK = 64
TOP_N = 16
WINDOW = 512
DIFF_KV_HEADS = 2
DIFF_GROUP = 2
DIFF_VDIM = 2 * HEAD_DIM
PAGE_SIZE = 128
EPS = 1e-6
NEG_INF = -1e30
MASK_BIAS = -(2.0 ** 100)
FORCE_SCORE = 1e9
QK_SCALE = HEAD_DIM ** -0.5

LANES = 128
POS_SPLIT = 256
SLOPE_TERMS = 3
LOG2E = math.log2(math.e)
NSA_W = NSA_HEADS * HEAD_DIM
KV_W = 4 * HEAD_DIM + 2 * DIFF_KV_HEADS * 2 * HEAD_DIM
WIN_W = 2 * HEAD_DIM
GL_W = 3 * NSA_HEADS
N_QD = DIFF_KV_HEADS * DIFF_GROUP * 2
N_KD = DIFF_KV_HEADS * 2
QD_W = N_QD * HEAD_DIM
DK_W = N_KD * HEAD_DIM
DV_W = DIFF_KV_HEADS * DIFF_VDIM

VMEM_LIMIT_BYTES = 56 * 1024 * 1024


def _cparams(n_grid_dims):
    return pltpu.CompilerParams(
        dimension_semantics=("arbitrary",) * n_grid_dims,
        vmem_limit_bytes=VMEM_LIMIT_BYTES,
    )


def _dot(a, b):
    return jnp.dot(a, b, preferred_element_type=F32)


def _dot_nt(a, b):
    return lax.dot_general(a, b, (((1,), (1,)), ((), ())), preferred_element_type=F32)


def _split_dot(x, mat):
    hi = x.astype(BF16)
    lo = (x - hi.astype(F32)).astype(BF16)
    return _dot(hi, mat) + _dot(lo, mat)


def _group_sumsq(x, gmat):
    return _split_dot(x * x, gmat)


def _block_sumsq(x, gmat):
    w = x.shape[1]
    if w <= gmat.shape[0]:
        return _group_sumsq(x, gmat[0:w, 0:w])
    cw = gmat.shape[0]
    return jnp.concatenate([_group_sumsq(x[:, c:c + cw], gmat) for c in range(0, w, cw)], axis=1)


def _lane_iota(shape):
    return lax.broadcasted_iota(jnp.int32, shape, len(shape) - 1)


def _row_iota(shape):
    return lax.broadcasted_iota(jnp.int32, shape, 0)


def _alibi_slopes(n):
    return [2.0 ** (-8.0 * (i + 1) / n) for i in range(n)]


def _row_const(rows_per_group, values):
    n = len(values)
    row = _row_iota((n * rows_per_group, 1))
    out = jnp.full((n * rows_per_group, 1), values[0], F32)
    for g in range(1, n):
        out = jnp.where(row >= g * rows_per_group, values[g], out)
    return out


def _pos_lanes(pos, width):
    lane = (_lane_iota((pos.shape[0], width)) & (LANES - 1)) - HEAD_DIM
    hi = (pos >> 8).astype(F32)
    lo = (pos & (POS_SPLIT - 1)).astype(F32)
    in_range = (lane >= 0) & (lane < 2 * SLOPE_TERMS)
    return jnp.where(in_range, jnp.where((lane & 1) == 0, hi, lo), 0.0)


def _online_step(s, v, m, l, acc):
    m_new = jnp.maximum(m, jnp.max(s, axis=-1, keepdims=True))
    alpha = jnp.exp(m - m_new)
    p = jnp.exp(s - m_new)
    l = alpha * l + jnp.sum(p, axis=-1, keepdims=True)
    acc = alpha * acc + _dot(p.astype(BF16), v)
    return m_new, l, acc


def _flash_init(m_s, l_s, acc_s):
    m_s[...] = jnp.full(m_s.shape, NEG_INF, F32)
    l_s[...] = jnp.zeros(l_s.shape, F32)
    acc_s[...] = jnp.zeros(acc_s.shape, F32)


def _flash_update(s, v, m_s, l_s, acc_s, r0):
    nrows, tk = s.shape
    sl = slice(r0, r0 + nrows)
    cols = [s[:, c:c + LANES] for c in range(0, tk, LANES)]
    smax = cols[0]
    for c in cols[1:]:
        smax = jnp.maximum(smax, c)
    m_old = m_s[sl]
    m_new = jnp.maximum(m_old, jnp.max(smax, axis=-1, keepdims=True))
    alpha = jnp.exp2(m_old - m_new)
    ps = [jnp.exp2(c - m_new) for c in cols]
    psum = ps[0]
    for p in ps[1:]:
        psum = psum + p
    l_s[sl] = alpha * l_s[sl] + psum
    p_bf = jnp.concatenate([p.astype(BF16) for p in ps], axis=1)
    acc_s[sl] = alpha * acc_s[sl] + _dot(p_bf, v)
    m_s[sl] = m_new


def _flash_finish(m_s, l_s, acc_s):
    l = jnp.sum(l_s[...], axis=-1, keepdims=True)
    return jnp.where(m_s[...] > 0.5 * NEG_INF, acc_s[...] / jnp.maximum(l, 1e-30), 0.0)


def _flash_loop(qk, consume, last):
    qk(0, 0)

    def body(j, carry):
        kt = 2 * j
        qk(kt + 1, 1)
        consume(kt, 0, False)
        qk(kt + 2, 0)
        consume(kt + 1, 1, False)
        return carry

    lax.fori_loop(0, last >> 1, body, 0)

    @pl.when((last & 1) == 1)
    def _():
        qk(last, 1)
        consume(last - 1, 0, False)
        consume(last, 1, True)

    @pl.when((last & 1) == 0)
    def _():
        consume(last, 0, True)


def _finish(m, l, acc):
    return jnp.where(m > 0.5 * NEG_INF, acc / jnp.maximum(l, 1e-30), 0.0)


def _compress(x, wcol):
    r = x.shape[0]
    return jnp.sum((x * wcol).reshape(r // CMP_BLOCK, CMP_BLOCK, x.shape[1]), axis=1)


def _norm_compressed(raw, kcg):
    is_k = _lane_iota(raw.shape) < HEAD_DIM
    ss = jnp.sum(jnp.where(is_k, raw * raw, 0.0), axis=-1, keepdims=True)
    return jnp.where(is_k, raw * lax.rsqrt(ss * (1.0 / HEAD_DIM) + EPS) * kcg, raw)


def _ada_kernel(c_ref, w_ref, b_ref, o_ref):
    c = c_ref[...]
    sc = c * jax.nn.sigmoid(c)
    o_ref[...] = _dot(sc.astype(BF16), w_ref[...].astype(BF16)) + b_ref[...]


def _ada(c, w_ada, b_ada):
    m, d = c.shape
    n = w_ada.shape[1]
    tn = d
    return pl.pallas_call(
        _ada_kernel,
        grid=(n // tn,),
        in_specs=[
            pl.BlockSpec((m, d), lambda j: (0, 0)),
            pl.BlockSpec((d, tn), lambda j: (0, j)),
            pl.BlockSpec((1, tn), lambda j: (0, j)),
        ],
        out_specs=pl.BlockSpec((m, tn), lambda j: (0, j)),
        out_shape=jax.ShapeDtypeStruct((m, n), F32),
        compiler_params=_cparams(1),
        name="ada",
    )(c, w_ada, b_ada.reshape(1, n))


_C_QN = 0
_C_QD = _C_QN + NSA_W
_C_KV = _C_QD + QD_W
_C_WIN = _C_KV + KV_W
_C_MG = _C_WIN + WIN_W


def _inproj_kernel(*refs, compress, steps_per_seq):
    (x_ref, sh_ref, sc_ref, g1_ref, w_ref, gmat_ref, qg_ref, qaug_ref, dqg_ref, dqaug_ref, selg_ref,
     wing_ref, dkg_ref) = refs[:13]
    n_in = 13
    if compress:
        wcol_ref, pe_ref, kcg_ref = refs[13:16]
        n_in = 16
    (kv_ref, win_ref, qn_ref, qd_ref, ksel_ref, kwin_ref, vnsa_ref, kd_ref, vd_ref, gl_ref,
     gm_ref) = refs[n_in:n_in + 11]
    tm, d_model = x_ref.shape
    c_gl = _C_MG + 2 * d_model

    x = x_ref[...]
    ms = jnp.mean(x * x, axis=-1, keepdims=True)
    h = x * lax.rsqrt(ms + EPS) * g1_ref[...]
    h = h * (1.0 + sc_ref[0]) + sh_ref[0]
    hb = h.astype(BF16)
    low = _lane_iota((tm, LANES)) < HEAD_DIM

    def proj(a, b):
        return _dot(hb, w_ref[:, a:b])

    def normed(z, gain):
        return z * lax.rsqrt(_block_sumsq(z, gmat_ref[...]) + EPS) * gain

    def spread(z):
        blocks = []
        for j in range(0, z.shape[1], LANES):
            pair = z[:, j:j + LANES]
            blocks.append(jnp.where(low, pair, 0.0))
            blocks.append(jnp.where(low, pltpu.roll(pair, HEAD_DIM, axis=1), 0.0))
        return jnp.concatenate(blocks, axis=1)

    def k_half(piece, gain):
        ms_k = _group_sumsq(piece, gmat_ref[0:LANES, 0:LANES])
        return jnp.where(low, piece * lax.rsqrt(ms_k + EPS) * gain, 0.0)

    qn_ref[...] = (spread(normed(proj(_C_QN, _C_QD), qg_ref[...])) + qaug_ref[...]).astype(BF16)
    qd_ref[...] = (spread(normed(proj(_C_QD, _C_KV), dqg_ref[...])) + dqaug_ref[...]).astype(BF16)

    pos = (pl.program_id(0) % steps_per_seq) * tm + _row_iota((tm, 1))
    kv = proj(_C_KV, _C_WIN)
    kv_ref[...] = kv
    win = proj(_C_WIN, _C_MG)
    win_ref[...] = win
    sel_piece = kv[:, 2 * HEAD_DIM:4 * HEAD_DIM]
    ksel = k_half(sel_piece, selg_ref[...]) + _pos_lanes(pos, LANES)
    block_onehot = jnp.where(_lane_iota((tm, LANES)) == (pos >> 6), 1.0, 0.0)
    ksel_ref[...] = jnp.concatenate([ksel, block_onehot], axis=1).astype(BF16)
    kwin_ref[...] = (k_half(win, wing_ref[...]) + _pos_lanes(pos, LANES)).astype(BF16)
    dk = kv[:, 4 * HEAD_DIM:4 * HEAD_DIM + DK_W]
    kd_ref[...] = (spread(normed(dk, dkg_ref[...])) + _pos_lanes(pos, N_KD * LANES)).astype(BF16)
    vd_ref[...] = kv[:, 4 * HEAD_DIM + DK_W:].astype(BF16)
    sel_v_first = pltpu.roll(sel_piece, HEAD_DIM, axis=1)
    vnsa_ref[...] = jnp.where(low, sel_v_first, win).astype(BF16)

    gm_ref[...] = jax.nn.sigmoid(proj(_C_MG, c_gl)).astype(gm_ref.dtype)
    gl_ref[...] = jax.nn.sigmoid(proj(c_gl, c_gl + LANES))

    if compress:
        kcvc_ref = refs[n_in + 11]
        raw = _compress(kv[:, 0:2 * HEAD_DIM] + pe_ref[...], wcol_ref[...])
        kcvc_ref[...] = _norm_compressed(raw, kcg_ref[...])


def _inproj(x2d, sh, sc, tm, p, seq, compress):
    n, d = x2d.shape
    steps = n // tm
    r = sh.shape[1]
    steps_per_group = steps // sh.shape[0]
    w = p["w_in"]

    def const(a):
        return pl.BlockSpec(a.shape, lambda i: (0,) * a.ndim, pipeline_mode=pl.Buffered(1))

    def rows(width):
        return pl.BlockSpec((tm, width), lambda i: (i, 0))

    mod_spec = pl.BlockSpec((1, r, d), lambda i: (i // steps_per_group, 0, 0))
    qg, dqg = (p["qg_log2"], p["dqg_log2"]) if compress else (p["qg"], p["dqg"])
    consts = [p["g1"], w, p["gmat64"], qg, p["qaug"], dqg, p["dqaug"], p["selg64"],
              p["wing64"], p["dkg64"]]
    ins = [x2d, sh, sc] + consts
    in_specs = [rows(d), mod_spec, mod_spec] + [const(a) for a in consts]
    outs = [(KV_W, F32), (WIN_W, F32), (NSA_HEADS * LANES, BF16), (N_QD * LANES, BF16),
            (2 * LANES, BF16), (LANES, BF16), (LANES, BF16), (N_KD * LANES, BF16), (DV_W, BF16),
            (LANES, F32), (2 * d, BF16)]
    out_shape = [jax.ShapeDtypeStruct((n, wd), dt) for wd, dt in outs]
    out_specs = [rows(wd) for wd, _ in outs]
    if compress:
        cins = [p["wcol"], p["pe"], p["kcg"]]
        ins += cins
        in_specs += [const(a) for a in cins]
        out_shape.append(jax.ShapeDtypeStruct((n // CMP_BLOCK, LANES), F32))
        out_specs.append(pl.BlockSpec((tm // CMP_BLOCK, LANES), lambda i: (i, 0)))
    return pl.pallas_call(
        functools.partial(_inproj_kernel, compress=compress, steps_per_seq=max(seq // tm, 1)),
        grid=(steps,),
        in_specs=in_specs,
        out_specs=out_specs,
        out_shape=out_shape,
        compiler_params=_cparams(1),
        name="inproj_cmp" if compress else "inproj",
    )(*ins)


def _stack_blocks(q, n):
    return jnp.concatenate([q[:, h * LANES:(h + 1) * LANES] for h in range(n)], axis=0)


def _pair_sum(x, n_out):
    n2 = x.shape[1]
    r = _row_iota((n2, n_out))
    c = _lane_iota((n2, n_out))
    pmat = jnp.where((r >> 1) == c, 1.0, 0.0).astype(BF16)
    hi = x.astype(BF16)
    r1 = x - hi.astype(F32)
    mid = r1.astype(BF16)
    lo = (r1 - mid.astype(F32)).astype(BF16)
    return _dot(hi, pmat) + _dot(mid, pmat) + _dot(lo, pmat)


def _top_blocks(score, n_top):
    work = score.T
    idx = _row_iota(work.shape).astype(F32)
    big = float(work.shape[0])
    sel = jnp.zeros_like(work)
    for _ in range(n_top):
        mx = jnp.max(work, axis=0, keepdims=True)
        first = jnp.min(jnp.where(work == mx, idx, big), axis=0, keepdims=True)
        hit = idx == first
        work, sel = jnp.where(hit, -2.0, work), jnp.where(hit, 1.0, sel)
    return sel.T


def _rank_select(score_row, n_top):
    n = score_row.shape[1]
    r = jnp.broadcast_to(score_row, (n, n))
    c = r.T
    i = _row_iota((n, n))
    j = _lane_iota((n, n))
    beats = (c > r) | ((c == r) & (i < j))
    rank = jnp.sum(jnp.where(beats, 1.0, 0.0), axis=0, keepdims=True)
    return rank < n_top


_ROW_CHUNK = 256


def _nsa_kernel(qn_ref, kcvc_ref, ksel_ref, kwin_ref, v_ref, gate_ref, o_ref, m_s, l_s, acc_s,
                s_buf, *, tq, tk, n_sel):
    qi = pl.program_id(1)
    nh = NSA_HEADS
    rows = nh * tq
    q0 = qi * tq
    qs = _stack_blocks(qn_ref[...], nh)
    qpos_t = q0 + _row_iota((tq, 1))

    def head(a, h):
        return a[h * tq:(h + 1) * tq]

    kcvc = kcvc_ref[...]
    n_cmp = kcvc.shape[0]
    cmp_end = _row_iota((n_cmp, 1)) * CMP_BLOCK + (CMP_BLOCK - 1)
    kc = jnp.where(_lane_iota(kcvc.shape) < HEAD_DIM, kcvc, _pos_lanes(cmp_end, LANES)).astype(BF16)
    ok_c = (_lane_iota((1, n_cmp)) * CMP_BLOCK + (CMP_BLOCK - 1)) <= qpos_t
    s_all = _dot_nt(qs, kc)
    p_heads = jnp.zeros((tq, n_cmp), F32)
    p_bf = []
    for h in range(nh):
        s = jnp.where(ok_c, head(s_all, h), NEG_INF)
        e = jnp.where(ok_c, jnp.exp2(s - jnp.max(s, axis=-1, keepdims=True)), 0.0)
        p = e / jnp.maximum(jnp.sum(e, axis=-1, keepdims=True), 1e-30)
        p_heads = p_heads + p
        p_bf.append(p.astype(BF16))
    o_c = _dot(jnp.concatenate(p_bf, axis=0), kcvc.astype(BF16))
    imp = _pair_sum(p_heads, LANES)
    blk = _lane_iota((1, LANES))
    forced = (blk == (qpos_t >> 6)) | (blk == 0)
    valid = blk * SEL_BLOCK <= qpos_t
    score = jnp.where(valid, jnp.where(forced, FORCE_SCORE, imp), -1.0)

    span = tq + WINDOW
    start = pl.multiple_of(jnp.maximum(q0 - WINDOW, 0), tq)
    kw = kwin_ref[pl.ds(start, span), :]
    vw = v_ref[pl.ds(start, span), :]
    dist = qpos_t - (start + _lane_iota((1, span)))
    ok_w = (dist >= 0) & (dist < WINDOW)
    s_all = _dot_nt(qs, kw)
    p_bf, l_w = [], []
    for h in range(nh):
        s = jnp.where(ok_w, head(s_all, h), NEG_INF)
        e = jnp.exp2(s - jnp.max(s, axis=-1, keepdims=True))
        l_w.append(jnp.sum(e, axis=-1, keepdims=True))
        p_bf.append(e.astype(BF16))
    o_w = _dot(jnp.concatenate(p_bf, axis=0), vw)

    sel = _top_blocks(score, min(TOP_N, n_sel))
    desel = jnp.where(valid & (sel > 0.5), 0.0, MASK_BIAS).astype(BF16)
    qfull = jnp.concatenate([qs, jnp.concatenate([desel] * nh, axis=0)], axis=1)

    _flash_init(m_s, l_s, acc_s)
    rc = min(_ROW_CHUNK, rows)

    def sel_qk(kt, slot):
        k = ksel_ref[pl.ds(pl.multiple_of(kt * tk, tk), tk), :]
        s_buf[slot] = _dot_nt(qfull, k)

    def sel_consume(kt, slot, causal):
        v = v_ref[pl.ds(pl.multiple_of(kt * tk, tk), tk), :]
        for r0 in range(0, rows, rc):
            s = s_buf[slot, r0:r0 + rc, :]
            if causal:
                qp = q0 + ((r0 + _row_iota((rc, 1))) & (tq - 1))
                s = jnp.where(kt * tk + _lane_iota((1, tk)) <= qp, s, NEG_INF)
            _flash_update(s, v, m_s, l_s, acc_s, r0)

    _flash_loop(sel_qk, sel_consume, q0 // tk)
    o_s = _flash_finish(m_s, l_s, acc_s)

    gate = gate_ref[...]
    low = _lane_iota((tq, LANES)) < HEAD_DIM
    outs = []
    for h in range(nh):
        upper = gate[:, 3 * h:3 * h + 1] * head(o_c, h) + gate[:, 3 * h + 2:3 * h + 3] * (head(o_w, h) / l_w[h])
        both = jnp.where(low, gate[:, 3 * h + 1:3 * h + 2] * head(o_s, h), upper)
        outs.append(both + pltpu.roll(both, HEAD_DIM, axis=1))
    for j in range(nh // 2):
        o_ref[:, j * LANES:(j + 1) * LANES] = jnp.where(low, outs[2 * j], outs[2 * j + 1]).astype(o_ref.dtype)


def _nsa_prompt(qn, kcvc, ksel, kwin, vnsa, gates, batch, seq, tq, tk):
    nq = seq // tq
    n_cmp = seq // CMP_BLOCK
    n_sel = seq // SEL_BLOCK
    assert n_sel <= LANES and seq % tk == 0 and tk % tq == 0 and seq >= tq + WINDOW
    rows = NSA_HEADS * tq

    def qrow(width):
        return pl.BlockSpec((tq, width), lambda b, i: (b * nq + i, 0))

    def per_batch(rows_, width):
        return pl.BlockSpec((rows_, width), lambda b, i: (b, 0))

    return pl.pallas_call(
        functools.partial(_nsa_kernel, tq=tq, tk=tk, n_sel=n_sel),
        grid=(batch, nq),
        in_specs=[qrow(NSA_HEADS * LANES), per_batch(n_cmp, LANES), per_batch(seq, 2 * LANES),
                  per_batch(seq, LANES), per_batch(seq, LANES), qrow(LANES)],
        out_specs=qrow(NSA_W),
        out_shape=jax.ShapeDtypeStruct((batch * seq, NSA_W), BF16),
        scratch_shapes=[pltpu.VMEM((rows, LANES), F32), pltpu.VMEM((rows, LANES), F32),
                        pltpu.VMEM((rows, LANES), F32), pltpu.VMEM((2, rows, tk), F32)],
        compiler_params=_cparams(2),
        name="nsa_prompt",
    )(qn, kcvc, ksel, kwin, vnsa, gates)


def _lambda_full(lam_ref, lam_init):
    dl = lam_ref[...]
    a = jnp.sum(dl[0:1] * dl[1:2], axis=-1, keepdims=True)
    b = jnp.sum(dl[2:3] * dl[3:4], axis=-1, keepdims=True)
    return jnp.exp(a) - jnp.exp(b) + lam_init


def _subln(o, gain, lam_init):
    ms = jnp.mean(o * o, axis=-1, keepdims=True)
    return o * lax.rsqrt(ms + EPS) * gain * (1.0 - lam_init)


def _diff_kernel(q_ref, k_ref, v_ref, lam_ref, subln_ref, o_ref, m_s, l_s, acc_s, s_buf, *, tq, tk,
                 lam_init):
    qi = pl.program_id(2)
    g = DIFF_GROUP
    half = g * tq
    rows = 2 * half
    q0 = qi * tq
    q = q_ref[...]
    qm = [jnp.concatenate([q[:, (gg * 2 + m) * LANES:(gg * 2 + m + 1) * LANES] for gg in range(g)],
                          axis=0) for m in range(2)]
    _flash_init(m_s, l_s, acc_s)
    rc = min(_ROW_CHUNK, half)

    def qk(kt, slot):
        k = k_ref[pl.ds(pl.multiple_of(kt * tk, tk), tk), :]
        for m in range(2):
            s_buf[slot, m * half:(m + 1) * half, :] = _dot_nt(qm[m], k[:, m * LANES:(m + 1) * LANES])

    def consume(kt, slot, causal):
        v = v_ref[pl.ds(pl.multiple_of(kt * tk, tk), tk), :]
        for r0 in range(0, rows, rc):
            s = s_buf[slot, r0:r0 + rc, :]
            if causal:
                qp = q0 + ((r0 + _row_iota((rc, 1))) & (tq - 1))
                s = jnp.where(kt * tk + _lane_iota((1, tk)) <= qp, s, NEG_INF)
            _flash_update(s, v, m_s, l_s, acc_s, r0)

    _flash_loop(qk, consume, q0 // tk)
    o = _flash_finish(m_s, l_s, acc_s)
    lam = _lambda_full(lam_ref, lam_init)
    for gg in range(g):
        r0 = gg * tq
        d = o[r0:r0 + tq] - lam * o[half + r0:half + r0 + tq]
        o_ref[:, gg * DIFF_VDIM:(gg + 1) * DIFF_VDIM] = _subln(d, subln_ref[...], lam_init).astype(o_ref.dtype)


def _diff_prompt(qd, kd, vd, lam, subln, batch, seq, tq, tk, lam_init):
    nq = seq // tq
    assert seq % tk == 0 and tk % tq == 0
    rows = 2 * DIFF_GROUP * tq
    return pl.pallas_call(
        functools.partial(_diff_kernel, tq=tq, tk=tk, lam_init=lam_init),
        grid=(batch, DIFF_KV_HEADS, nq),
        in_specs=[
            pl.BlockSpec((tq, DIFF_GROUP * 2 * LANES), lambda b, n, i: (b * nq + i, n)),
            pl.BlockSpec((seq, 2 * LANES), lambda b, n, i: (b, n)),
            pl.BlockSpec((seq, DIFF_VDIM), lambda b, n, i: (b, n)),
            pl.BlockSpec(lam.shape, lambda b, n, i: (0, 0)),
            pl.BlockSpec(subln.shape, lambda b, n, i: (0, 0)),
        ],
        out_specs=pl.BlockSpec((tq, DIFF_GROUP * DIFF_VDIM), lambda b, n, i: (b * nq + i, n)),
        out_shape=jax.ShapeDtypeStruct((batch * seq, QD_W), BF16),
        scratch_shapes=[pltpu.VMEM((rows, LANES), F32), pltpu.VMEM((rows, LANES), F32),
                        pltpu.VMEM((rows, DIFF_VDIM), F32), pltpu.VMEM((2, rows, tk), F32)],
        compiler_params=_cparams(3),
        name="diff_prompt",
    )(qd, kd, vd, lam, subln)


_QROWS = 16
_PAGES_PER_STEP = 16


def _decode_kernel(pt_ref, *refs, n_chunks, lam_init):
    del pt_ref
    npg = _PAGES_PER_STEP
    pages = refs[:npg]
    (qn_ref, qd_ref, selkv_new_ref, winkv_new_ref, kd_new_ref, vd_new_ref, gate_ref, win_ref,
     gmat_ref, wcol_ref, pe_ref, kcg_ref, selg_ref, wing_ref, dkg_ref, lam_ref,
     subln_ref) = refs[npg:npg + 17]
    oa_ref, ob_ref = refs[npg + 17:npg + 19]
    (cmp_buf, kd_a, vd_a, kd_b, vd_b, stash, sel_buf, kcvc_s, dm_s, dl_s, dacc_s) = refs[npg + 19:]
    c = pl.program_id(1)
    rows_per_step = npg * PAGE_SIZE
    past_len = n_chunks * rows_per_step
    nh = NSA_HEADS

    slopes = _alibi_slopes(DIFF_KV_HEADS * DIFF_GROUP)
    dslope = _row_const(1, [slopes[r // 2] for r in range(8)] + [0.0] * (_QROWS - 8))
    qd = qd_ref[...]
    qd_gained = (qd.astype(F32) * dkg_ref[...]).astype(BF16)

    def normalise(kd_buf, vd_buf):
        for i in range(npg):
            pg = pages[i]
            r0 = i * PAGE_SIZE
            cmp_buf[r0:r0 + PAGE_SIZE, :] = pg[:, 0:2 * HEAD_DIM]
            sk = pg[:, 2 * HEAD_DIM:4 * HEAD_DIM]
            ms = _dot((sk * sk).astype(BF16), gmat_ref[0:LANES, 0:LANES])
            r = jnp.where(_lane_iota(sk.shape) < HEAD_DIM, lax.rsqrt(ms + EPS), 1.0)
            base = pl.multiple_of(c * rows_per_step + r0, PAGE_SIZE)
            stash[pl.ds(base, PAGE_SIZE), :] = (sk * r).astype(BF16)
            dk = pg[:, 4 * HEAD_DIM:4 * HEAD_DIM + DK_W]
            ms = _dot((dk * dk).astype(BF16), gmat_ref[...])
            kd_buf[r0:r0 + PAGE_SIZE, :] = (dk * lax.rsqrt(ms + EPS)).astype(BF16)
            vd_buf[r0:r0 + PAGE_SIZE, :] = pg[:, 4 * HEAD_DIM + DK_W:].astype(BF16)
        raw = _compress(cmp_buf[...] + pe_ref[...], wcol_ref[...])
        cmp_rows = rows_per_step // CMP_BLOCK
        kcvc_s[pl.ds(pl.multiple_of(c * cmp_rows, cmp_rows), cmp_rows), :] = _norm_compressed(
            raw, kcg_ref[...])

    def attend(kd_buf, vd_buf, chunk, live):
        kpos = chunk * rows_per_step + _lane_iota((1, rows_per_step))
        sc = _dot_nt(qd_gained, kd_buf[...]) - dslope * (past_len - kpos).astype(F32)
        m_old = dm_s[...]
        m_new = jnp.where(live, jnp.maximum(m_old, jnp.max(sc, axis=-1, keepdims=True)), m_old)
        alpha = jnp.exp(m_old - m_new)
        p = jnp.where(live, jnp.exp(sc - m_new), 0.0)
        dl_s[...] = alpha * dl_s[...] + jnp.sum(p, axis=-1, keepdims=True)
        dacc_s[...] = alpha * dacc_s[...] + _dot(p.astype(BF16), vd_buf[...])
        dm_s[...] = m_new

    @pl.when(c == 0)
    def _():
        dm_s[...] = jnp.full(dm_s.shape, NEG_INF, F32)
        dl_s[...] = jnp.zeros(dl_s.shape, F32)
        dacc_s[...] = jnp.zeros(dacc_s.shape, F32)
        kd_b[...] = jnp.zeros(kd_b.shape, BF16)
        vd_b[...] = jnp.zeros(vd_b.shape, BF16)

    @pl.when((c & 1) == 0)
    def _():
        attend(kd_b, vd_b, c - 1, c > 0)
        normalise(kd_a, vd_a)

    @pl.when((c & 1) == 1)
    def _():
        attend(kd_a, vd_a, c - 1, True)
        normalise(kd_b, vd_b)

    @pl.when(c == n_chunks - 1)
    def _():
        if (n_chunks - 1) % 2 == 0:
            attend(kd_a, vd_a, c, True)
        else:
            attend(kd_b, vd_b, c, True)
        dm, dl, dacc = dm_s[...], dl_s[...], dacc_s[...]
        kd_new = kd_new_ref[...].astype(F32)
        s_new = jnp.sum(qd.astype(F32) * kd_new, axis=-1, keepdims=True)
        m2 = jnp.maximum(dm, s_new)
        alpha = jnp.exp(dm - m2)
        p_new = jnp.exp(s_new - m2)
        l2 = alpha * dl + p_new
        acc2 = alpha * dacc + p_new * vd_new_ref[...].astype(F32)
        od = acc2 / l2
        lam = _lambda_full(lam_ref, lam_init)
        for n in range(DIFF_KV_HEADS):
            for g in range(DIFF_GROUP):
                r = (n * DIFF_GROUP + g) * 2
                lo, hi = n * DIFF_VDIM, (n + 1) * DIFF_VDIM
                d = od[r:r + 1, lo:hi] - lam * od[r + 1:r + 2, lo:hi]
                col = (n * DIFF_GROUP + g) * DIFF_VDIM
                ob_ref[:, col:col + DIFF_VDIM] = _subln(d, subln_ref[...], lam_init).astype(ob_ref.dtype)

        qn = qn_ref[...]
        aslope = _row_const(1, _alibi_slopes(nh) + [0.0] * (_QROWS - nh))
        kcvc = kcvc_s[...]
        n_cmp = kcvc.shape[0]
        cmp_end = _lane_iota((1, n_cmp)) * CMP_BLOCK + (CMP_BLOCK - 1)
        s = _dot_nt(qn, kcvc[:, :HEAD_DIM].astype(BF16)) - aslope * (past_len - cmp_end).astype(F32)
        e = jnp.exp(s - jnp.max(s, axis=-1, keepdims=True))
        p_c = e / jnp.sum(e, axis=-1, keepdims=True)
        o_c = _dot(p_c.astype(BF16), kcvc[:, HEAD_DIM:].astype(BF16))
        head_row = _row_iota((_QROWS, 1)) < nh
        p_sum = jnp.sum(jnp.where(head_row, p_c, 0.0), axis=0, keepdims=True)
        n_blk = n_cmp // 2
        imp = _pair_sum(jnp.broadcast_to(p_sum, (8, n_cmp)), LANES)[0:1]
        blk = _lane_iota((1, LANES))
        score = jnp.where(blk == 0, FORCE_SCORE, jnp.where(blk < n_blk, imp, -1.0))
        n_pick = min(TOP_N - 1, n_blk)
        sel = _rank_select(score, n_pick)
        r = _row_iota((LANES, LANES))
        upper = jnp.where(r < _lane_iota((LANES, LANES)), 1.0, 0.0).astype(BF16)
        sel8 = jnp.broadcast_to(jnp.where(sel, 1.0, 0.0), (8, LANES)).astype(BF16)
        slot_of = _dot(sel8, upper)[0:1]
        slot = _row_iota((_QROWS, 1)).astype(F32)
        block_of = jnp.sum(jnp.where(sel & (slot_of == slot), blk.astype(F32), 0.0),
                           axis=-1, keepdims=True).astype(jnp.int32)
        lane_slot = _lane_iota((1, _QROWS * SEL_BLOCK)) >> 6
        kpos = _lane_iota((1, _QROWS * SEL_BLOCK)) & (SEL_BLOCK - 1)
        for t in range(n_pick):
            b_t = block_of[t, 0]
            src = pl.multiple_of(b_t * SEL_BLOCK, SEL_BLOCK)
            sel_buf[t * SEL_BLOCK:(t + 1) * SEL_BLOCK, :] = stash[pl.ds(src, SEL_BLOCK), :]
            kpos = jnp.where(lane_slot == t, kpos + b_t * SEL_BLOCK, kpos)
        sel_buf[n_pick * SEL_BLOCK:, :] = jnp.zeros(((_QROWS - n_pick) * SEL_BLOCK, LANES), BF16)
        kv = sel_buf[...]
        qn_gained = (qn.astype(F32) * selg_ref[:, 0:HEAD_DIM]).astype(BF16)
        sc = _dot_nt(qn_gained, kv[:, :HEAD_DIM]) - aslope * (past_len - kpos).astype(F32)
        sc = jnp.where(lane_slot < n_pick, sc, NEG_INF)

        init = (jnp.full((_QROWS, 1), NEG_INF, F32), jnp.zeros((_QROWS, 1), F32),
                jnp.zeros((_QROWS, HEAD_DIM), F32))
        sm, sl_, sacc = _online_step(sc, kv[:, HEAD_DIM:], *init)
        qf = qn.astype(F32)

        def add_new(m, l, acc, kv_new):
            s_new = jnp.sum(qf * kv_new[:, :HEAD_DIM], axis=-1, keepdims=True)
            m2 = jnp.maximum(m, s_new)
            alpha = jnp.exp(m - m2)
            p_new = jnp.exp(s_new - m2)
            return (alpha * acc + p_new * kv_new[:, HEAD_DIM:]) / (alpha * l + p_new)

        o_s = add_new(sm, sl_, sacc, selkv_new_ref[...].astype(F32))

        win = win_ref[...]
        w_buf = win.shape[0]
        ms = _group_sumsq(win, gmat_ref[0:LANES, 0:LANES])
        r = jnp.where(_lane_iota(win.shape) < HEAD_DIM, lax.rsqrt(ms + EPS), 1.0)
        wkv = (win * r * wing_ref[...]).astype(BF16)
        dist_w = w_buf - _lane_iota((1, w_buf))
        sc = _dot_nt(qn, wkv[:, :HEAD_DIM]) - aslope * dist_w.astype(F32)
        sc = jnp.where(dist_w < WINDOW, sc, NEG_INF)
        o_w = add_new(*_online_step(sc, wkv[:, HEAD_DIM:], *init), winkv_new_ref[...].astype(F32))

        gate = gate_ref[...]
        for h in range(nh):
            o = (gate[:, 3 * h:3 * h + 1] * o_c[h:h + 1] + gate[:, 3 * h + 1:3 * h + 2] * o_s[h:h + 1]
                 + gate[:, 3 * h + 2:3 * h + 3] * o_w[h:h + 1])
            oa_ref[:, h * HEAD_DIM:(h + 1) * HEAD_DIM] = o.astype(oa_ref.dtype)


def _decode(l, page_table, cache, qn16, qd16, selkv_new, winkv_new, kd_new, vd_new, gates,
            win_state, p, lam_init):
    nb, n_pages = page_table.shape
    npg = _PAGES_PER_STEP
    n_chunks = n_pages // npg
    rows_per_step = npg * PAGE_SIZE
    past_len = n_pages * PAGE_SIZE

    def page_spec(i):
        return pl.BlockSpec((None, None, PAGE_SIZE, KV_W),
                            lambda b, c, pt: (l, pt[b, c * npg + i], 0, 0))

    def per_seq(a):
        if a.ndim == 4:
            return pl.BlockSpec((None, None) + a.shape[2:], lambda b, c, pt: (l, b, 0, 0))
        return pl.BlockSpec((None,) + a.shape[1:], lambda b, c, pt: (b,) + (0,) * (a.ndim - 1))

    def const(a):
        return pl.BlockSpec(a.shape, lambda b, c, pt: (0,) * a.ndim)

    seq_ins = [qn16, qd16, selkv_new, winkv_new, kd_new, vd_new, gates, win_state]
    const_ins = [p["gmat64"], p["wcol_dec"], p["pe_dec"], p["kcg"], p["selg64"], p["wing64"],
                 p["dkg64"], p["lam"], p["subln"]]
    grid_spec = pltpu.PrefetchScalarGridSpec(
        num_scalar_prefetch=1,
        grid=(nb, n_chunks),
        in_specs=[page_spec(i) for i in range(npg)] + [per_seq(a) for a in seq_ins]
        + [const(a) for a in const_ins],
        out_specs=[pl.BlockSpec((None, 1, NSA_W), lambda b, c, pt: (b, 0, 0)),
                   pl.BlockSpec((None, 1, QD_W), lambda b, c, pt: (b, 0, 0))],
        scratch_shapes=[
            pltpu.VMEM((rows_per_step, LANES), F32),
            pltpu.VMEM((rows_per_step, DK_W), BF16),
            pltpu.VMEM((rows_per_step, DV_W), BF16),
            pltpu.VMEM((rows_per_step, DK_W), BF16),
            pltpu.VMEM((rows_per_step, DV_W), BF16),
            pltpu.VMEM((past_len, LANES), BF16),
            pltpu.VMEM((_QROWS * SEL_BLOCK, LANES), BF16),
            pltpu.VMEM((past_len // CMP_BLOCK, LANES), F32),
            pltpu.VMEM((_QROWS, 1), F32),
            pltpu.VMEM((_QROWS, 1), F32),
            pltpu.VMEM((_QROWS, DV_W), F32),
        ],
    )
    return pl.pallas_call(
        functools.partial(_decode_kernel, n_chunks=n_chunks, lam_init=lam_init),
        grid_spec=grid_spec,
        out_shape=[jax.ShapeDtypeStruct((nb, 1, NSA_W), BF16),
                   jax.ShapeDtypeStruct((nb, 1, QD_W), BF16)],
        compiler_params=_cparams(2),
        name="decode",
    )(page_table, *([cache] * npg), *seq_ins, *const_ins)


def _merge_kernel(oa_ref, ob_ref, gm_ref, x_ref, ga_ref, wa_ref, wb_ref, wo_ref, o_ref):
    d = x_ref.shape[-1]
    ya = _dot(oa_ref[...], wa_ref[...])
    yb = _dot(ob_ref[...], wb_ref[...])
    mg = gm_ref[:, 0:d] * ya + gm_ref[:, d:2 * d] * yb
    o_ref[...] = x_ref[...] + ga_ref[0] * _dot(mg.astype(BF16), wo_ref[...])


def _merge(oa, ob, gm, x2d, ga, tm, p):
    n, d = x2d.shape
    steps = n // tm
    r = ga.shape[1]
    steps_per_group = steps // ga.shape[0]

    def rows(width):
        return pl.BlockSpec((tm, width), lambda i: (i, 0))

    def const(a):
        return pl.BlockSpec(a.shape, lambda i: (0,) * a.ndim)

    return pl.pallas_call(
        _merge_kernel,
        grid=(steps,),
        in_specs=[rows(NSA_W), rows(QD_W), rows(2 * d), rows(d),
                  pl.BlockSpec((1, r, d), lambda i: (i // steps_per_group, 0, 0)),
                  const(p["w_br_nsa"]), const(p["w_br_diff"]), const(p["w_out"])],
        out_specs=rows(d),
        out_shape=jax.ShapeDtypeStruct((n, d), F32),
        compiler_params=_cparams(1),
        name="merge",
    )(oa, ob, gm, x2d, ga, p["w_br_nsa"], p["w_br_diff"], p["w_out"])


def _ffn_kernel(x_ref, sh_ref, sc_ref, ga_ref, g2_ref, wi_ref, wo_ref, o_ref):
    x = x_ref[...]
    d_ff = wo_ref.shape[0]
    ms = jnp.mean(x * x, axis=-1, keepdims=True)
    h = x * lax.rsqrt(ms + EPS) * g2_ref[...]
    hb = (h * (1.0 + sc_ref[0]) + sh_ref[0]).astype(BF16)
    u = _dot(hb, wi_ref[...])
    ug = u[:, 0:d_ff]
    act = (ug * jax.nn.sigmoid(ug) * u[:, d_ff:]).astype(BF16)
    o_ref[...] = x + ga_ref[0] * _dot(act, wo_ref[...])


def _ffn(x2d, sh, sc, ga, tm, p):
    n, d = x2d.shape
    steps = n // tm
    r = sh.shape[1]
    steps_per_group = steps // sh.shape[0]
    rows = pl.BlockSpec((tm, d), lambda i: (i, 0))
    mod_spec = pl.BlockSpec((1, r, d), lambda i: (i // steps_per_group, 0, 0))

    def const(a):
        return pl.BlockSpec(a.shape, lambda i: (0,) * a.ndim, pipeline_mode=pl.Buffered(1))

    return pl.pallas_call(
        _ffn_kernel,
        grid=(steps,),
        in_specs=[rows, mod_spec, mod_spec, mod_spec, const(p["g2"]), const(p["w_ffn_in"]),
                  const(p["w_ffn_out"])],
        out_specs=rows,
        out_shape=jax.ShapeDtypeStruct((n, d), F32),
        compiler_params=_cparams(1),
        name="ffn",
    )(x2d, sh, sc, ga, p["g2"], p["w_ffn_in"], p["w_ffn_out"])


def _tiled_gain(g, reps, scale=1.0):
    return (jnp.tile(g.astype(F32), reps) * scale).reshape(1, -1)


def _slope_lanes(slopes):
    out = jnp.zeros((len(slopes), LANES), F32)
    rest = jnp.asarray(slopes, F32) * LOG2E
    for t in range(SLOPE_TERMS):
        term = rest.astype(BF16).astype(F32)
        rest = rest - term
        out = out.at[:, HEAD_DIM + 2 * t].set(term * POS_SPLIT).at[:, HEAD_DIM + 2 * t + 1].set(term)
    return out.reshape(1, -1)


def _half_gain(g):
    return jnp.concatenate([g.astype(F32), jnp.ones((HEAD_DIM,), F32)]).reshape(1, -1)


def _compress_weights(wk, wv, rows):
    col = jnp.concatenate([jnp.tile(wk.astype(F32)[:, None], (1, HEAD_DIM)),
                           jnp.tile(wv.astype(F32)[:, None], (1, HEAD_DIM))], axis=1)
    return jnp.tile(col, (rows // CMP_BLOCK, 1))


def _prepare_params(l, tm_prompt, w_ada, b_ada, norm1_g, norm2_g, w_in, nsa_q_norm, nsa_k_norm,
                    nsa_cmp_wk, nsa_cmp_wv, nsa_cmp_pe, diff_q_norm, diff_k_norm, diff_lambda,
                    diff_subln, w_br_nsa, w_br_diff, w_out, w_ffn_in, w_ffn_out):
    d = w_in.shape[1]
    wi = w_in[l]
    o_kv = NSA_W
    o_win = o_kv + KV_W
    o_gl = o_win + WIN_W
    o_qd = o_gl + GL_W
    o_mg = o_qd + QD_W
    w_r = jnp.concatenate([
        wi[:, 0:NSA_W], wi[:, o_qd:o_mg], wi[:, o_kv:o_win], wi[:, o_win:o_gl],
        wi[:, o_mg:o_mg + 2 * d], wi[:, o_gl:o_qd],
        jnp.zeros((d, LANES - GL_W), wi.dtype)], axis=1).astype(BF16)
    pe2 = jnp.concatenate([nsa_cmp_pe[l].astype(F32), jnp.zeros((CMP_BLOCK, HEAD_DIM), F32)], axis=1)
    dec_rows = _PAGES_PER_STEP * PAGE_SIZE
    nsa_slopes = _alibi_slopes(NSA_HEADS)
    diff_slopes = _alibi_slopes(DIFF_KV_HEADS * DIFF_GROUP)
    g64 = jnp.arange(2 * LANES) // HEAD_DIM
    return {
        "w_ada": w_ada[l], "b_ada": b_ada[l],
        "g1": norm1_g[l].astype(F32).reshape(1, -1), "g2": norm2_g[l].astype(F32).reshape(1, -1),
        "w_in": w_r,
        "gmat64": ((g64[:, None] == g64[None, :]) * (1.0 / HEAD_DIM)).astype(BF16),
        "qg": _tiled_gain(nsa_q_norm[l], NSA_HEADS, QK_SCALE),
        "qg_log2": _tiled_gain(nsa_q_norm[l], NSA_HEADS, QK_SCALE * LOG2E),
        "qaug": _slope_lanes(nsa_slopes),
        "dqg": _tiled_gain(diff_q_norm[l], N_QD, QK_SCALE),
        "dqg_log2": _tiled_gain(diff_q_norm[l], N_QD, QK_SCALE * LOG2E),
        "dqaug": _slope_lanes([diff_slopes[j // 2] for j in range(N_QD)]),
        "kcg": _half_gain(nsa_k_norm[l, 0]),
        "selg64": _half_gain(nsa_k_norm[l, 1]), "wing64": _half_gain(nsa_k_norm[l, 2]),
        "dkg64": _tiled_gain(diff_k_norm[l], N_KD),
        "wcol": _compress_weights(nsa_cmp_wk[l], nsa_cmp_wv[l], tm_prompt),
        "pe": jnp.tile(pe2, (tm_prompt // CMP_BLOCK, 1)),
        "wcol_dec": _compress_weights(nsa_cmp_wk[l], nsa_cmp_wv[l], dec_rows),
        "pe_dec": jnp.tile(pe2, (dec_rows // CMP_BLOCK, 1)),
        "lam": diff_lambda[l].astype(F32), "subln": diff_subln[l].astype(F32).reshape(1, -1),
        "w_br_nsa": w_br_nsa[l].astype(BF16), "w_br_diff": w_br_diff[l].astype(BF16),
        "w_out": w_out[l].astype(BF16),
        "w_ffn_in": w_ffn_in[l].astype(BF16), "w_ffn_out": w_ffn_out[l].astype(BF16),
    }


def _pad_rows(a, rows):
    return jnp.concatenate([a, jnp.zeros((a.shape[0], rows - a.shape[1]) + a.shape[2:], a.dtype)], axis=1)


def _unblock(a, n):
    return a.reshape(a.shape[0], n, LANES)[:, :, :HEAD_DIM]


def _decode_queries(qn, qd):
    nb = qn.shape[0]
    qn16 = _pad_rows(qn, _QROWS)
    q = qd.reshape(nb, DIFF_KV_HEADS, DIFF_GROUP, 2, HEAD_DIM)
    slot = jnp.arange(N_KD).reshape(DIFF_KV_HEADS, 1, 2)
    onehot = (slot[..., None] == jnp.arange(N_KD)).astype(q.dtype)
    qmat = q[:, :, :, :, None, :] * onehot[None, :, :, :, :, None]
    qd16 = _pad_rows(qmat.reshape(nb, N_QD, DK_W), _QROWS)
    return qn16, qd16


def _layer(l, xp, xs, cache_kv, state_win_kv, page_table, mod, p, tm, tiles):
    batch, seq, d = xp.shape
    nb = xs.shape[0]
    win_l = state_win_kv[l]
    lam_init = 0.8 - 0.6 * math.exp(-0.3 * l)
    mod_p = mod[:batch].reshape(batch, 1, 6 * d)
    mod_s = mod[batch:batch + nb].reshape(1, nb, 6 * d)

    def part(m, k):
        return m[:, :, k * d:(k + 1) * d]

    x2 = xp.reshape(batch * seq, d)
    (kv, win, qn, qd, ksel, kwin, vnsa, kd, vd, gl, gm, kcvc) = _inproj(
        x2, part(mod_p, 0), part(mod_p, 1), tm, p, seq, compress=True)
    oa = _nsa_prompt(qn, kcvc, ksel, kwin, vnsa, gl, batch, seq, tiles["nsa_tq"], tiles["nsa_tk"])
    ob = _diff_prompt(qd, kd, vd, p["lam"], p["subln"], batch, seq, tiles["diff_tq"],
                      tiles["diff_tk"], lam_init)
    x1 = _merge(oa, ob, gm, x2, part(mod_p, 2), tm, p)
    yp = _ffn(x1, part(mod_p, 3), part(mod_p, 4), part(mod_p, 5), tm, p)
    w_buf = win_l.shape[1]
    win_p = win.reshape(batch, seq, WIN_W)[:, seq - w_buf:]

    xs2 = xs.reshape(nb, d)
    (kv_s, win_s, qn_s, qd_s, ksel_s, kwin_s, vnsa_s, kd_s, vd_s, gl_s, gm_s) = _inproj(
        xs2, part(mod_s, 0), part(mod_s, 1), nb, p, 1, compress=False)
    qn16, qd16 = _decode_queries(_unblock(qn_s, NSA_HEADS), _unblock(qd_s, N_QD))
    selkv_s = jnp.concatenate([ksel_s[:, :HEAD_DIM], vnsa_s[:, :HEAD_DIM]], axis=1)
    winkv_s = jnp.concatenate([kwin_s[:, :HEAD_DIM], vnsa_s[:, HEAD_DIM:]], axis=1)
    kd_plain = _unblock(kd_s, N_KD).reshape(nb, DK_W)
    oa_s, ob_s = _decode(l, page_table, cache_kv, qn16, qd16, selkv_s[:, None], winkv_s[:, None],
                         kd_plain[:, None], vd_s[:, None], gl_s[:, None], state_win_kv, p, lam_init)
    x1s = _merge(oa_s.reshape(nb, NSA_W), ob_s.reshape(nb, QD_W), gm_s, xs2, part(mod_s, 2), nb, p)
    ys = _ffn(x1s, part(mod_s, 3), part(mod_s, 4), part(mod_s, 5), nb, p)
    win_state_s = jnp.concatenate([win_l[:, 1:], win_s[:, None]], axis=1)
    return (yp.reshape(batch, seq, d), ys.reshape(nb, 1, d), kv.reshape(batch, seq, KV_W),
            kv_s.reshape(nb, 1, KV_W), win_p, win_state_s)


def _tiles(seq):
    return {"nsa_tq": min(256, seq), "nsa_tk": min(512, seq), "diff_tq": min(256, seq),
            "diff_tk": min(512, seq)}


def kernel(x_prompt, x_sample, cache_kv, state_win_kv, page_table, c_prompt, c_sample, w_ada, b_ada,
           norm1_g, norm2_g, w_in, nsa_q_norm, nsa_k_norm, nsa_cmp_wk, nsa_cmp_wv, nsa_cmp_pe,
           diff_q_norm, diff_k_norm, diff_lambda, diff_subln, w_br_nsa, w_br_diff, w_out, w_ffn_in,
           w_ffn_out):
    depth = w_in.shape[0]
    seq = x_prompt.shape[1]
    assert x_sample.shape[1] == 1 and cache_kv.shape[2] == PAGE_SIZE
    tm = min(512, seq)
    xp, xs = x_prompt, x_sample
    kv_p, kv_s, win_p, win_s = [], [], [], []
    n_cond = c_prompt.shape[0] + c_sample.shape[0]
    c_all = jnp.concatenate([c_prompt, c_sample,
                             jnp.zeros((-n_cond % 16, c_prompt.shape[1]), c_prompt.dtype)], axis=0)
    for l in range(depth):
        p = _prepare_params(l, tm, w_ada, b_ada, norm1_g, norm2_g, w_in, nsa_q_norm, nsa_k_norm,
                            nsa_cmp_wk, nsa_cmp_wv, nsa_cmp_pe, diff_q_norm, diff_k_norm,
                            diff_lambda, diff_subln, w_br_nsa, w_br_diff, w_out, w_ffn_in, w_ffn_out)
        mod = _ada(c_all, p["w_ada"], p["b_ada"])
        xp, xs, kvp, kvs, wp, ws = _layer(l, xp, xs, cache_kv, state_win_kv, page_table, mod, p, tm,
                                          _tiles(seq))
        kv_p.append(kvp)
        kv_s.append(kvs)
        win_p.append(wp)
        win_s.append(ws)
    return (xp, xs, jnp.stack(kv_p), jnp.stack(kv_s), jnp.stack(win_p), jnp.stack(win_s))
```

```python
import functools
import math

import jax
import jax.numpy as jnp
from jax import lax
from jax.experimental import pallas as pl
from jax.experimental.pallas import tpu as pltpu

F32 = jnp.float32
BF16 = jnp.bfloat16

HEAD_DIM = 64
NSA_HEADS = 8
CMP_BLOCK = 32
SEL_BLOCK = 64
TOP_N = 16
WINDOW = 512
DIFF_KV_HEADS = 2
DIFF_GROUP = 2
DIFF_VDIM = 2 * HEAD_DIM
PAGE_SIZE = 128
EPS = 1e-6
NEG_INF = -1e30
MASK_BIAS = -(2.0 ** 100)
FORCE_SCORE = 1e9
QK_SCALE = HEAD_DIM ** -0.5

LANES = 128
POS_SPLIT = 256
SLOPE_TERMS = 3
LOG2E = math.log2(math.e)
NSA_W = NSA_HEADS * HEAD_DIM
KV_W = 4 * HEAD_DIM + 2 * DIFF_KV_HEADS * 2 * HEAD_DIM
WIN_W = 2 * HEAD_DIM
GL_W = 3 * NSA_HEADS
N_QD = DIFF_KV_HEADS * DIFF_GROUP * 2
N_KD = DIFF_KV_HEADS * 2
QD_W = N_QD * HEAD_DIM
DK_W = N_KD * HEAD_DIM
DV_W = DIFF_KV_HEADS * DIFF_VDIM

VMEM_LIMIT_BYTES = 56 * 1024 * 1024


def _cparams(n_grid_dims):
    return pltpu.CompilerParams(
        dimension_semantics=("arbitrary",) * n_grid_dims,
        vmem_limit_bytes=VMEM_LIMIT_BYTES,
    )


def _dot(a, b):
    return jnp.dot(a, b, preferred_element_type=F32)


def _dot_nt(a, b):
    return lax.dot_general(a, b, (((1,), (1,)), ((), ())), preferred_element_type=F32)


def _split_dot(x, mat):
    hi = x.astype(BF16)
    lo = (x - hi.astype(F32)).astype(BF16)
    return _dot(hi, mat) + _dot(lo, mat)


def _group_sumsq(x, gmat):
    return _split_dot(x * x, gmat)


def _block_sumsq(x, gmat):
    w = x.shape[1]
    if w <= gmat.shape[0]:
        return _group_sumsq(x, gmat[0:w, 0:w])
    cw = gmat.shape[0]
    return jnp.concatenate([_group_sumsq(x[:, c:c + cw], gmat) for c in range(0, w, cw)], axis=1)


def _lane_iota(shape):
    return lax.broadcasted_iota(jnp.int32, shape, len(shape) - 1)


def _row_iota(shape):
    return lax.broadcasted_iota(jnp.int32, shape, 0)


def _alibi_slopes(n):
    return [2.0 ** (-8.0 * (i + 1) / n) for i in range(n)]


def _row_const(rows_per_group, values):
    n = len(values)
    row = _row_iota((n * rows_per_group, 1))
    out = jnp.full((n * rows_per_group, 1), values[0], F32)
    for g in range(1, n):
        out = jnp.where(row >= g * rows_per_group, values[g], out)
    return out


def _pos_lanes(pos, width):
    lane = (_lane_iota((pos.shape[0], width)) & (LANES - 1)) - HEAD_DIM
    hi = (pos >> 8).astype(F32)
    lo = (pos & (POS_SPLIT - 1)).astype(F32)
    in_range = (lane >= 0) & (lane < 2 * SLOPE_TERMS)
    return jnp.where(in_range, jnp.where((lane & 1) == 0, hi, lo), 0.0)


def _online_step(s, v, m, l, acc):
    m_new = jnp.maximum(m, jnp.max(s, axis=-1, keepdims=True))
    alpha = jnp.exp(m - m_new)
    p = jnp.exp(s - m_new)
    l = alpha * l + jnp.sum(p, axis=-1, keepdims=True)
    acc = alpha * acc + _dot(p.astype(BF16), v)
    return m_new, l, acc


def _flash_init(m_s, l_s, acc_s):
    m_s[...] = jnp.full(m_s.shape, NEG_INF, F32)
    l_s[...] = jnp.zeros(l_s.shape, F32)
    acc_s[...] = jnp.zeros(acc_s.shape, F32)


def _flash_update(load_col, n_cols, v, m_s, l_s, acc_s, r0, nrows):
    sl = slice(r0, r0 + nrows)
    smax = load_col(0)
    for c in range(1, n_cols):
        smax = jnp.maximum(smax, load_col(c))
    m_old = m_s[sl]
    m_new = jnp.maximum(m_old, jnp.max(smax, axis=-1, keepdims=True))
    alpha = jnp.exp2(m_old - m_new)
    acc_s[sl] = alpha * acc_s[sl]
    psum = alpha * l_s[sl]
    p_bf = []
    for c in range(n_cols):
        p = jnp.exp2(load_col(c) - m_new)
        psum = psum + p
        p_bf.append(p.astype(BF16))
    l_s[sl] = psum
    m_s[sl] = m_new
    acc_s[sl] += _dot(jnp.concatenate(p_bf, axis=1), v)


def _score_loader(s_buf, slot, r0, nrows, k0, q0, tq, causal):
    def load_col(c):
        s = s_buf[slot, r0:r0 + nrows, c * LANES:(c + 1) * LANES]
        if causal:
            qp = q0 + ((r0 + _row_iota((nrows, 1))) & (tq - 1))
            s = jnp.where(k0 + c * LANES + _lane_iota((1, LANES)) <= qp, s, NEG_INF)
        return s
    return load_col


def _flash_finish(m_s, l_s, acc_s):
    l = jnp.sum(l_s[...], axis=-1, keepdims=True)
    return jnp.where(m_s[...] > 0.5 * NEG_INF, acc_s[...] / jnp.maximum(l, 1e-30), 0.0)


def _flash_loop(qk, consume, last):
    qk(0, 0)

    def body(j, carry):
        kt = 2 * j
        qk(kt + 1, 1)
        consume(kt, 0, False)
        qk(kt + 2, 0)
        consume(kt + 1, 1, False)
        return carry

    lax.fori_loop(0, last >> 1, body, 0)

    @pl.when((last & 1) == 1)
    def _():
        qk(last, 1)
        consume(last - 1, 0, False)
        consume(last, 1, True)

    @pl.when((last & 1) == 0)
    def _():
        consume(last, 0, True)


def _finish(m, l, acc):
    return jnp.where(m > 0.5 * NEG_INF, acc / jnp.maximum(l, 1e-30), 0.0)


def _compress(x, wcol):
    r = x.shape[0]
    return jnp.sum((x * wcol).reshape(r // CMP_BLOCK, CMP_BLOCK, x.shape[1]), axis=1)


def _norm_compressed(raw, kcg):
    is_k = _lane_iota(raw.shape) < HEAD_DIM
    ss = jnp.sum(jnp.where(is_k, raw * raw, 0.0), axis=-1, keepdims=True)
    return jnp.where(is_k, raw * lax.rsqrt(ss * (1.0 / HEAD_DIM) + EPS) * kcg, raw)


def _ada_kernel(c_ref, w_ref, b_ref, o_ref):
    c = c_ref[...]
    sc = c * jax.nn.sigmoid(c)
    o_ref[...] = _dot(sc.astype(BF16), w_ref[...].astype(BF16)) + b_ref[...]


def _ada(c, w_ada, b_ada):
    m, d = c.shape
    n = w_ada.shape[1]
    tn = d
    return pl.pallas_call(
        _ada_kernel,
        grid=(n // tn,),
        in_specs=[
            pl.BlockSpec((m, d), lambda j: (0, 0)),
            pl.BlockSpec((d, tn), lambda j: (0, j)),
            pl.BlockSpec((1, tn), lambda j: (0, j)),
        ],
        out_specs=pl.BlockSpec((m, tn), lambda j: (0, j)),
        out_shape=jax.ShapeDtypeStruct((m, n), F32),
        compiler_params=_cparams(1),
        name="ada",
    )(c, w_ada, b_ada.reshape(1, n))


_C_QN = 0
_C_QD = _C_QN + NSA_W
_C_KV = _C_QD + QD_W
_C_WIN = _C_KV + KV_W
_C_MG = _C_WIN + WIN_W


def _inproj_kernel(*refs, compress, steps_per_seq):
    (x_ref, sh_ref, sc_ref, g1_ref, w_ref, gmat_ref, qg_ref, qaug_ref, dqg_ref, dqaug_ref, selg_ref,
     wing_ref, dkg_ref) = refs[:13]
    n_in = 13
    if compress:
        wcol_ref, pe_ref, kcg_ref = refs[13:16]
        n_in = 16
    (kv_ref, win_ref, qn_ref, qd_ref, ksel_ref, kwin_ref, vnsa_ref, kd_ref, vd_ref, gl_ref,
     gm_ref) = refs[n_in:n_in + 11]
    tm, d_model = x_ref.shape
    c_gl = _C_MG + 2 * d_model

    x = x_ref[...]
    ms = jnp.mean(x * x, axis=-1, keepdims=True)
    h = x * lax.rsqrt(ms + EPS) * g1_ref[...]
    h = h * (1.0 + sc_ref[0]) + sh_ref[0]
    hb = h.astype(BF16)
    low = _lane_iota((tm, LANES)) < HEAD_DIM

    def proj(a, b):
        return _dot(hb, w_ref[:, a:b])

    def normed(z, gain):
        return z * lax.rsqrt(_block_sumsq(z, gmat_ref[...]) + EPS) * gain

    def spread(z):
        blocks = []
        for j in range(0, z.shape[1], LANES):
            pair = z[:, j:j + LANES]
            blocks.append(jnp.where(low, pair, 0.0))
            blocks.append(jnp.where(low, pltpu.roll(pair, HEAD_DIM, axis=1), 0.0))
        return jnp.concatenate(blocks, axis=1)

    def k_half(piece, gain):
        ms_k = _group_sumsq(piece, gmat_ref[0:LANES, 0:LANES])
        return jnp.where(low, piece * lax.rsqrt(ms_k + EPS) * gain, 0.0)

    qn_ref[...] = (spread(normed(proj(_C_QN, _C_QD), qg_ref[...])) + qaug_ref[...]).astype(BF16)
    qd_ref[...] = (spread(normed(proj(_C_QD, _C_KV), dqg_ref[...])) + dqaug_ref[...]).astype(BF16)

    pos = (pl.program_id(0) % steps_per_seq) * tm + _row_iota((tm, 1))
    kv = proj(_C_KV, _C_WIN)
    kv_ref[...] = kv
    win = proj(_C_WIN, _C_MG)
    win_ref[...] = win
    sel_piece = kv[:, 2 * HEAD_DIM:4 * HEAD_DIM]
    ksel = k_half(sel_piece, selg_ref[...]) + _pos_lanes(pos, LANES)
    block_onehot = jnp.where(_lane_iota((tm, LANES)) == (pos >> 6), 1.0, 0.0)
    ksel_ref[...] = jnp.concatenate([ksel, block_onehot], axis=1).astype(BF16)
    kwin_ref[...] = (k_half(win, wing_ref[...]) + _pos_lanes(pos, LANES)).astype(BF16)
    dk = kv[:, 4 * HEAD_DIM:4 * HEAD_DIM + DK_W]
    kd_ref[...] = (spread(normed(dk, dkg_ref[...])) + _pos_lanes(pos, N_KD * LANES)).astype(BF16)
    vd_ref[...] = kv[:, 4 * HEAD_DIM + DK_W:].astype(BF16)
    sel_v_first = pltpu.roll(sel_piece, HEAD_DIM, axis=1)
    vnsa_ref[...] = jnp.where(low, sel_v_first, win).astype(BF16)

    gm_ref[...] = jax.nn.sigmoid(proj(_C_MG, c_gl)).astype(gm_ref.dtype)
    gl_ref[...] = jax.nn.sigmoid(proj(c_gl, c_gl + LANES))

    if compress:
        kcvc_ref = refs[n_in + 11]
        raw = _compress(kv[:, 0:2 * HEAD_DIM] + pe_ref[...], wcol_ref[...])
        kcvc_ref[...] = _norm_compressed(raw, kcg_ref[...])


def _inproj(x2d, sh, sc, tm, p, seq, compress):
    n, d = x2d.shape
    steps = n // tm
    r = sh.shape[1]
    steps_per_group = steps // sh.shape[0]
    w = p["w_in"]

    def const(a):
        return pl.BlockSpec(a.shape, lambda i: (0,) * a.ndim, pipeline_mode=pl.Buffered(1))

    def rows(width):
        return pl.BlockSpec((tm, width), lambda i: (i, 0))

    mod_spec = pl.BlockSpec((1, r, d), lambda i: (i // steps_per_group, 0, 0))
    qg, dqg = (p["qg_log2"], p["dqg_log2"]) if compress else (p["qg"], p["dqg"])
    consts = [p["g1"], w, p["gmat64"], qg, p["qaug"], dqg, p["dqaug"], p["selg64"],
              p["wing64"], p["dkg64"]]
    ins = [x2d, sh, sc] + consts
    in_specs = [rows(d), mod_spec, mod_spec] + [const(a) for a in consts]
    outs = [(KV_W, F32), (WIN_W, F32), (NSA_HEADS * LANES, BF16), (N_QD * LANES, BF16),
            (2 * LANES, BF16), (LANES, BF16), (LANES, BF16), (N_KD * LANES, BF16), (DV_W, BF16),
            (LANES, F32), (2 * d, BF16)]
    out_shape = [jax.ShapeDtypeStruct((n, wd), dt) for wd, dt in outs]
    out_specs = [rows(wd) for wd, _ in outs]
    if compress:
        cins = [p["wcol"], p["pe"], p["kcg"]]
        ins += cins
        in_specs += [const(a) for a in cins]
        out_shape.append(jax.ShapeDtypeStruct((n // CMP_BLOCK, LANES), F32))
        out_specs.append(pl.BlockSpec((tm // CMP_BLOCK, LANES), lambda i: (i, 0)))
    return pl.pallas_call(
        functools.partial(_inproj_kernel, compress=compress, steps_per_seq=max(seq // tm, 1)),
        grid=(steps,),
        in_specs=in_specs,
        out_specs=out_specs,
        out_shape=out_shape,
        compiler_params=_cparams(1),
        name="inproj_cmp" if compress else "inproj",
    )(*ins)


def _stack_blocks(q, n):
    return jnp.concatenate([q[:, h * LANES:(h + 1) * LANES] for h in range(n)], axis=0)


def _pair_sum(x, n_out):
    n2 = x.shape[1]
    r = _row_iota((n2, n_out))
    c = _lane_iota((n2, n_out))
    pmat = jnp.where((r >> 1) == c, 1.0, 0.0).astype(BF16)
    hi = x.astype(BF16)
    r1 = x - hi.astype(F32)
    mid = r1.astype(BF16)
    lo = (r1 - mid.astype(F32)).astype(BF16)
    return _dot(hi, pmat) + _dot(mid, pmat) + _dot(lo, pmat)


def _top_blocks(score, n_top):
    work = score.T
    idx = _row_iota(work.shape).astype(F32)
    big = float(work.shape[0])
    sel = jnp.zeros_like(work)
    for _ in range(n_top):
        mx = jnp.max(work, axis=0, keepdims=True)
        first = jnp.min(jnp.where(work == mx, idx, big), axis=0, keepdims=True)
        hit = idx == first
        work, sel = jnp.where(hit, -2.0, work), jnp.where(hit, 1.0, sel)
    return sel.T


def _rank_select(score_row, n_top):
    n = score_row.shape[1]
    r = jnp.broadcast_to(score_row, (n, n))
    c = r.T
    i = _row_iota((n, n))
    j = _lane_iota((n, n))
    beats = (c > r) | ((c == r) & (i < j))
    rank = jnp.sum(jnp.where(beats, 1.0, 0.0), axis=0, keepdims=True)
    return rank < n_top


_ROW_CHUNK = 256


def _nsa_kernel(qn_ref, kcvc_ref, ksel_ref, kwin_ref, v_ref, gate_ref, o_ref, m_s, l_s, acc_s,
                s_buf, *, tq, tk, n_sel):
    qi = pl.program_id(1)
    nh = NSA_HEADS
    rows = nh * tq
    q0 = qi * tq
    qs = _stack_blocks(qn_ref[...], nh)
    qpos_t = q0 + _row_iota((tq, 1))

    def head(a, h):
        return a[h * tq:(h + 1) * tq]

    kcvc = kcvc_ref[...]
    n_cmp = kcvc.shape[0]
    cmp_end = _row_iota((n_cmp, 1)) * CMP_BLOCK + (CMP_BLOCK - 1)
    kc = jnp.where(_lane_iota(kcvc.shape) < HEAD_DIM, kcvc, _pos_lanes(cmp_end, LANES)).astype(BF16)
    ok_c = (_lane_iota((1, n_cmp)) * CMP_BLOCK + (CMP_BLOCK - 1)) <= qpos_t
    s_all = _dot_nt(qs, kc)
    p_heads = jnp.zeros((tq, n_cmp), F32)
    p_bf = []
    for h in range(nh):
        s = jnp.where(ok_c, head(s_all, h), NEG_INF)
        e = jnp.where(ok_c, jnp.exp2(s - jnp.max(s, axis=-1, keepdims=True)), 0.0)
        p = e / jnp.maximum(jnp.sum(e, axis=-1, keepdims=True), 1e-30)
        p_heads = p_heads + p
        p_bf.append(p.astype(BF16))
    o_c = _dot(jnp.concatenate(p_bf, axis=0), kcvc.astype(BF16))
    imp = _pair_sum(p_heads, LANES)
    blk = _lane_iota((1, LANES))
    forced = (blk == (qpos_t >> 6)) | (blk == 0)
    valid = blk * SEL_BLOCK <= qpos_t
    score = jnp.where(valid, jnp.where(forced, FORCE_SCORE, imp), -1.0)

    span = tq + WINDOW
    start = pl.multiple_of(jnp.maximum(q0 - WINDOW, 0), tq)
    kw = kwin_ref[pl.ds(start, span), :]
    vw = v_ref[pl.ds(start, span), :]
    dist = qpos_t - (start + _lane_iota((1, span)))
    ok_w = (dist >= 0) & (dist < WINDOW)
    s_all = _dot_nt(qs, kw)
    p_bf, l_w = [], []
    for h in range(nh):
        s = jnp.where(ok_w, head(s_all, h), NEG_INF)
        e = jnp.exp2(s - jnp.max(s, axis=-1, keepdims=True))
        l_w.append(jnp.sum(e, axis=-1, keepdims=True))
        p_bf.append(e.astype(BF16))
    o_w = _dot(jnp.concatenate(p_bf, axis=0), vw)

    sel = _top_blocks(score, min(TOP_N, n_sel))
    desel = jnp.where(valid & (sel > 0.5), 0.0, MASK_BIAS).astype(BF16)
    qfull = jnp.concatenate([qs, jnp.concatenate([desel] * nh, axis=0)], axis=1)

    _flash_init(m_s, l_s, acc_s)
    rc = min(_ROW_CHUNK, rows)

    def sel_qk(kt, slot):
        k = ksel_ref[pl.ds(pl.multiple_of(kt * tk, tk), tk), :]
        s_buf[slot] = _dot_nt(qfull, k)

    def sel_consume(kt, slot, causal):
        v = v_ref[pl.ds(pl.multiple_of(kt * tk, tk), tk), :]
        for r0 in range(0, rows, rc):
            _flash_update(_score_loader(s_buf, slot, r0, rc, kt * tk, q0, tq, causal), tk // LANES,
                          v, m_s, l_s, acc_s, r0, rc)

    _flash_loop(sel_qk, sel_consume, q0 // tk)
    o_s = _flash_finish(m_s, l_s, acc_s)

    gate = gate_ref[...]
    low = _lane_iota((tq, LANES)) < HEAD_DIM
    outs = []
    for h in range(nh):
        upper = gate[:, 3 * h:3 * h + 1] * head(o_c, h) + gate[:, 3 * h + 2:3 * h + 3] * (head(o_w, h) / l_w[h])
        both = jnp.where(low, gate[:, 3 * h + 1:3 * h + 2] * head(o_s, h), upper)
        outs.append(both + pltpu.roll(both, HEAD_DIM, axis=1))
    for j in range(nh // 2):
        o_ref[:, j * LANES:(j + 1) * LANES] = jnp.where(low, outs[2 * j], outs[2 * j + 1]).astype(o_ref.dtype)


def _nsa_prompt(qn, kcvc, ksel, kwin, vnsa, gates, batch, seq, tq, tk):
    nq = seq // tq
    n_cmp = seq // CMP_BLOCK
    n_sel = seq // SEL_BLOCK
    assert n_sel <= LANES and seq % tk == 0 and tk % tq == 0 and seq >= tq + WINDOW
    rows = NSA_HEADS * tq

    def qrow(width):
        return pl.BlockSpec((tq, width), lambda b, i: (b * nq + i, 0))

    def per_batch(rows_, width):
        return pl.BlockSpec((rows_, width), lambda b, i: (b, 0))

    return pl.pallas_call(
        functools.partial(_nsa_kernel, tq=tq, tk=tk, n_sel=n_sel),
        grid=(batch, nq),
        in_specs=[qrow(NSA_HEADS * LANES), per_batch(n_cmp, LANES), per_batch(seq, 2 * LANES),
                  per_batch(seq, LANES), per_batch(seq, LANES), qrow(LANES)],
        out_specs=qrow(NSA_W),
        out_shape=jax.ShapeDtypeStruct((batch * seq, NSA_W), BF16),
        scratch_shapes=[pltpu.VMEM((rows, LANES), F32), pltpu.VMEM((rows, LANES), F32),
                        pltpu.VMEM((rows, LANES), F32), pltpu.VMEM((2, rows, tk), F32)],
        compiler_params=_cparams(2),
        name="nsa_prompt",
    )(qn, kcvc, ksel, kwin, vnsa, gates)


def _lambda_full(lam_ref, lam_init):
    dl = lam_ref[...]
    a = jnp.sum(dl[0:1] * dl[1:2], axis=-1, keepdims=True)
    b = jnp.sum(dl[2:3] * dl[3:4], axis=-1, keepdims=True)
    return jnp.exp(a) - jnp.exp(b) + lam_init


def _subln(o, gain, lam_init):
    ms = jnp.mean(o * o, axis=-1, keepdims=True)
    return o * lax.rsqrt(ms + EPS) * gain * (1.0 - lam_init)


def _diff_kernel(q_ref, k_ref, v_ref, lam_ref, subln_ref, o_ref, m_s, l_s, acc_s, s_buf, *, tq, tk,
                 lam_init):
    qi = pl.program_id(2)
    g = DIFF_GROUP
    half = g * tq
    rows = 2 * half
    q0 = qi * tq
    q = q_ref[...]
    qm = [jnp.concatenate([q[:, (gg * 2 + m) * LANES:(gg * 2 + m + 1) * LANES] for gg in range(g)],
                          axis=0) for m in range(2)]
    _flash_init(m_s, l_s, acc_s)
    rc = min(_ROW_CHUNK, half)

    def qk(kt, slot):
        k = k_ref[pl.ds(pl.multiple_of(kt * tk, tk), tk), :]
        for m in range(2):
            s_buf[slot, m * half:(m + 1) * half, :] = _dot_nt(qm[m], k[:, m * LANES:(m + 1) * LANES])

    def consume(kt, slot, causal):
        v = v_ref[pl.ds(pl.multiple_of(kt * tk, tk), tk), :]
        for r0 in range(0, rows, rc):
            _flash_update(_score_loader(s_buf, slot, r0, rc, kt * tk, q0, tq, causal), tk // LANES,
                          v, m_s, l_s, acc_s, r0, rc)

    _flash_loop(qk, consume, q0 // tk)
    o = _flash_finish(m_s, l_s, acc_s)
    lam = _lambda_full(lam_ref, lam_init)
    for gg in range(g):
        r0 = gg * tq
        d = o[r0:r0 + tq] - lam * o[half + r0:half + r0 + tq]
        o_ref[:, gg * DIFF_VDIM:(gg + 1) * DIFF_VDIM] = _subln(d, subln_ref[...], lam_init).astype(o_ref.dtype)


def _diff_prompt(qd, kd, vd, lam, subln, batch, seq, tq, tk, lam_init):
    nq = seq // tq
    assert seq % tk == 0 and tk % tq == 0
    rows = 2 * DIFF_GROUP * tq
    return pl.pallas_call(
        functools.partial(_diff_kernel, tq=tq, tk=tk, lam_init=lam_init),
        grid=(batch, DIFF_KV_HEADS, nq),
        in_specs=[
            pl.BlockSpec((tq, DIFF_GROUP * 2 * LANES), lambda b, n, i: (b * nq + i, n)),
            pl.BlockSpec((seq, 2 * LANES), lambda b, n, i: (b, n)),
            pl.BlockSpec((seq, DIFF_VDIM), lambda b, n, i: (b, n)),
            pl.BlockSpec(lam.shape, lambda b, n, i: (0, 0)),
            pl.BlockSpec(subln.shape, lambda b, n, i: (0, 0)),
        ],
        out_specs=pl.BlockSpec((tq, DIFF_GROUP * DIFF_VDIM), lambda b, n, i: (b * nq + i, n)),
        out_shape=jax.ShapeDtypeStruct((batch * seq, QD_W), BF16),
        scratch_shapes=[pltpu.VMEM((rows, LANES), F32), pltpu.VMEM((rows, LANES), F32),
                        pltpu.VMEM((rows, DIFF_VDIM), F32), pltpu.VMEM((2, rows, tk), F32)],
        compiler_params=_cparams(3),
        name="diff_prompt",
    )(qd, kd, vd, lam, subln)


_QROWS = 16
_PAGES_PER_STEP = 16


def _decode_kernel(pt_ref, *refs, n_chunks, lam_init):
    del pt_ref
    npg = _PAGES_PER_STEP
    pages = refs[:npg]
    (qn_ref, qd_ref, selkv_new_ref, winkv_new_ref, kd_new_ref, vd_new_ref, gate_ref, win_ref,
     gmat_ref, wcol_ref, pe_ref, kcg_ref, selg_ref, wing_ref, dkg_ref, lam_ref,
     subln_ref) = refs[npg:npg + 17]
    oa_ref, ob_ref = refs[npg + 17:npg + 19]
    (cmp_buf, kd_a, vd_a, kd_b, vd_b, stash, sel_buf, kcvc_s, dm_s, dl_s, dacc_s) = refs[npg + 19:]
    c = pl.program_id(1)
    rows_per_step = npg * PAGE_SIZE
    past_len = n_chunks * rows_per_step
    nh = NSA_HEADS

    slopes = _alibi_slopes(DIFF_KV_HEADS * DIFF_GROUP)
    dslope = _row_const(1, [slopes[r // 2] for r in range(8)] + [0.0] * (_QROWS - 8))
    qd = qd_ref[...]
    qd_gained = (qd.astype(F32) * dkg_ref[...]).astype(BF16)

    def normalise(kd_buf, vd_buf):
        for i in range(npg):
            pg = pages[i]
            r0 = i * PAGE_SIZE
            cmp_buf[r0:r0 + PAGE_SIZE, :] = pg[:, 0:2 * HEAD_DIM]
            sk = pg[:, 2 * HEAD_DIM:4 * HEAD_DIM]
            ms = _dot((sk * sk).astype(BF16), gmat_ref[0:LANES, 0:LANES])
            r = jnp.where(_lane_iota(sk.shape) < HEAD_DIM, lax.rsqrt(ms + EPS), 1.0)
            base = pl.multiple_of(c * rows_per_step + r0, PAGE_SIZE)
            stash[pl.ds(base, PAGE_SIZE), :] = (sk * r).astype(BF16)
            dk = pg[:, 4 * HEAD_DIM:4 * HEAD_DIM + DK_W]
            ms = _dot((dk * dk).astype(BF16), gmat_ref[...])
            kd_buf[r0:r0 + PAGE_SIZE, :] = (dk * lax.rsqrt(ms + EPS)).astype(BF16)
            vd_buf[r0:r0 + PAGE_SIZE, :] = pg[:, 4 * HEAD_DIM + DK_W:].astype(BF16)
        raw = _compress(cmp_buf[...] + pe_ref[...], wcol_ref[...])
        cmp_rows = rows_per_step // CMP_BLOCK
        kcvc_s[pl.ds(pl.multiple_of(c * cmp_rows, cmp_rows), cmp_rows), :] = _norm_compressed(
            raw, kcg_ref[...])

    def attend(kd_buf, vd_buf, chunk, live):
        kpos = chunk * rows_per_step + _lane_iota((1, rows_per_step))
        half = rows_per_step // 2
        sc = jnp.concatenate([_dot_nt(qd_gained, kd_buf[0:half, :]),
                              _dot_nt(qd_gained, kd_buf[half:, :])], axis=1)
        sc = sc - dslope * (past_len - kpos).astype(F32)
        m_old = dm_s[...]
        m_new = jnp.where(live, jnp.maximum(m_old, jnp.max(sc, axis=-1, keepdims=True)), m_old)
        alpha = jnp.exp(m_old - m_new)
        p = jnp.where(live, jnp.exp(sc - m_new), 0.0)
        dl_s[...] = alpha * dl_s[...] + jnp.sum(p, axis=-1, keepdims=True)
        pb = p.astype(BF16)
        dacc_s[...] = (alpha * dacc_s[...] + _dot(pb[:, 0:half], vd_buf[0:half, :])
                       + _dot(pb[:, half:], vd_buf[half:, :]))
        dm_s[...] = m_new

    @pl.when(c == 0)
    def _():
        dm_s[...] = jnp.full(dm_s.shape, NEG_INF, F32)
        dl_s[...] = jnp.zeros(dl_s.shape, F32)
        dacc_s[...] = jnp.zeros(dacc_s.shape, F32)
        kd_b[...] = jnp.zeros(kd_b.shape, BF16)
        vd_b[...] = jnp.zeros(vd_b.shape, BF16)

    @pl.when((c & 1) == 0)
    def _():
        attend(kd_b, vd_b, c - 1, c > 0)
        normalise(kd_a, vd_a)

    @pl.when((c & 1) == 1)
    def _():
        attend(kd_a, vd_a, c - 1, True)
        normalise(kd_b, vd_b)

    @pl.when(c == n_chunks - 1)
    def _():
        if (n_chunks - 1) % 2 == 0:
            attend(kd_a, vd_a, c, True)
        else:
            attend(kd_b, vd_b, c, True)
        dm, dl, dacc = dm_s[...], dl_s[...], dacc_s[...]
        kd_new = kd_new_ref[...].astype(F32)
        s_new = jnp.sum(qd.astype(F32) * kd_new, axis=-1, keepdims=True)
        m2 = jnp.maximum(dm, s_new)
        alpha = jnp.exp(dm - m2)
        p_new = jnp.exp(s_new - m2)
        l2 = alpha * dl + p_new
        acc2 = alpha * dacc + p_new * vd_new_ref[...].astype(F32)
        od = acc2 / l2
        lam = _lambda_full(lam_ref, lam_init)
        for n in range(DIFF_KV_HEADS):
            for g in range(DIFF_GROUP):
                r = (n * DIFF_GROUP + g) * 2
                lo, hi = n * DIFF_VDIM, (n + 1) * DIFF_VDIM
                d = od[r:r + 1, lo:hi] - lam * od[r + 1:r + 2, lo:hi]
                col = (n * DIFF_GROUP + g) * DIFF_VDIM
                ob_ref[:, col:col + DIFF_VDIM] = _subln(d, subln_ref[...], lam_init).astype(ob_ref.dtype)

        qn = qn_ref[...]
        aslope = _row_const(1, _alibi_slopes(nh) + [0.0] * (_QROWS - nh))
        kcvc = kcvc_s[...]
        n_cmp = kcvc.shape[0]
        cmp_end = _lane_iota((1, n_cmp)) * CMP_BLOCK + (CMP_BLOCK - 1)
        s = _dot_nt(qn, kcvc[:, :HEAD_DIM].astype(BF16)) - aslope * (past_len - cmp_end).astype(F32)
        e = jnp.exp(s - jnp.max(s, axis=-1, keepdims=True))
        p_c = e / jnp.sum(e, axis=-1, keepdims=True)
        o_c = _dot(p_c.astype(BF16), kcvc[:, HEAD_DIM:].astype(BF16))
        head_row = _row_iota((_QROWS, 1)) < nh
        p_sum = jnp.sum(jnp.where(head_row, p_c, 0.0), axis=0, keepdims=True)
        n_blk = n_cmp // 2
        imp = _pair_sum(jnp.broadcast_to(p_sum, (8, n_cmp)), LANES)[0:1]
        blk = _lane_iota((1, LANES))
        score = jnp.where(blk == 0, FORCE_SCORE, jnp.where(blk < n_blk, imp, -1.0))
        n_pick = min(TOP_N - 1, n_blk)
        sel = _rank_select(score, n_pick)
        r = _row_iota((LANES, LANES))
        upper = jnp.where(r < _lane_iota((LANES, LANES)), 1.0, 0.0).astype(BF16)
        sel8 = jnp.broadcast_to(jnp.where(sel, 1.0, 0.0), (8, LANES)).astype(BF16)
        slot_of = _dot(sel8, upper)[0:1]
        slot = _row_iota((_QROWS, 1)).astype(F32)
        block_of = jnp.sum(jnp.where(sel & (slot_of == slot), blk.astype(F32), 0.0),
                           axis=-1, keepdims=True).astype(jnp.int32)
        lane_slot = _lane_iota((1, _QROWS * SEL_BLOCK)) >> 6
        kpos = _lane_iota((1, _QROWS * SEL_BLOCK)) & (SEL_BLOCK - 1)
        for t in range(n_pick):
            b_t = block_of[t, 0]
            src = pl.multiple_of(b_t * SEL_BLOCK, SEL_BLOCK)
            sel_buf[t * SEL_BLOCK:(t + 1) * SEL_BLOCK, :] = stash[pl.ds(src, SEL_BLOCK), :]
            kpos = jnp.where(lane_slot == t, kpos + b_t * SEL_BLOCK, kpos)
        sel_buf[n_pick * SEL_BLOCK:, :] = jnp.zeros(((_QROWS - n_pick) * SEL_BLOCK, LANES), BF16)
        kv = sel_buf[...]
        qn_gained = (qn.astype(F32) * selg_ref[:, 0:HEAD_DIM]).astype(BF16)
        sc = _dot_nt(qn_gained, kv[:, :HEAD_DIM]) - aslope * (past_len - kpos).astype(F32)
        sc = jnp.where(lane_slot < n_pick, sc, NEG_INF)

        init = (jnp.full((_QROWS, 1), NEG_INF, F32), jnp.zeros((_QROWS, 1), F32),
                jnp.zeros((_QROWS, HEAD_DIM), F32))
        sm, sl_, sacc = _online_step(sc, kv[:, HEAD_DIM:], *init)
        qf = qn.astype(F32)

        def add_new(m, l, acc, kv_new):
            s_new = jnp.sum(qf * kv_new[:, :HEAD_DIM], axis=-1, keepdims=True)
            m2 = jnp.maximum(m, s_new)
            alpha = jnp.exp(m - m2)
            p_new = jnp.exp(s_new - m2)
            return (alpha * acc + p_new * kv_new[:, HEAD_DIM:]) / (alpha * l + p_new)

        o_s = add_new(sm, sl_, sacc, selkv_new_ref[...].astype(F32))

        win = win_ref[...]
        w_buf = win.shape[0]
        ms = _group_sumsq(win, gmat_ref[0:LANES, 0:LANES])
        r = jnp.where(_lane_iota(win.shape) < HEAD_DIM, lax.rsqrt(ms + EPS), 1.0)
        wkv = (win * r * wing_ref[...]).astype(BF16)
        dist_w = w_buf - _lane_iota((1, w_buf))
        sc = _dot_nt(qn, wkv[:, :HEAD_DIM]) - aslope * dist_w.astype(F32)
        sc = jnp.where(dist_w < WINDOW, sc, NEG_INF)
        o_w = add_new(*_online_step(sc, wkv[:, HEAD_DIM:], *init), winkv_new_ref[...].astype(F32))

        gate = gate_ref[...]
        for h in range(nh):
            o = (gate[:, 3 * h:3 * h + 1] * o_c[h:h + 1] + gate[:, 3 * h + 1:3 * h + 2] * o_s[h:h + 1]
                 + gate[:, 3 * h + 2:3 * h + 3] * o_w[h:h + 1])
            oa_ref[:, h * HEAD_DIM:(h + 1) * HEAD_DIM] = o.astype(oa_ref.dtype)


def _decode(l, page_table, cache, qn16, qd16, selkv_new, winkv_new, kd_new, vd_new, gates,
            win_state, p, lam_init):
    nb, n_pages = page_table.shape
    npg = _PAGES_PER_STEP
    n_chunks = n_pages // npg
    rows_per_step = npg * PAGE_SIZE
    past_len = n_pages * PAGE_SIZE

    def page_spec(i):
        return pl.BlockSpec((None, None, PAGE_SIZE, KV_W),
                            lambda b, c, pt: (l, pt[b, c * npg + i], 0, 0))

    def per_seq(a):
        if a.ndim == 4:
            return pl.BlockSpec((None, None) + a.shape[2:], lambda b, c, pt: (l, b, 0, 0))
        return pl.BlockSpec((None,) + a.shape[1:], lambda b, c, pt: (b,) + (0,) * (a.ndim - 1))

    def const(a):
        return pl.BlockSpec(a.shape, lambda b, c, pt: (0,) * a.ndim)

    seq_ins = [qn16, qd16, selkv_new, winkv_new, kd_new, vd_new, gates, win_state]
    const_ins = [p["gmat64"], p["wcol_dec"], p["pe_dec"], p["kcg"], p["selg64"], p["wing64"],
                 p["dkg64"], p["lam"], p["subln"]]
    grid_spec = pltpu.PrefetchScalarGridSpec(
        num_scalar_prefetch=1,
        grid=(nb, n_chunks),
        in_specs=[page_spec(i) for i in range(npg)] + [per_seq(a) for a in seq_ins]
        + [const(a) for a in const_ins],
        out_specs=[pl.BlockSpec((None, 1, NSA_W), lambda b, c, pt: (b, 0, 0)),
                   pl.BlockSpec((None, 1, QD_W), lambda b, c, pt: (b, 0, 0))],
        scratch_shapes=[
            pltpu.VMEM((rows_per_step, LANES), F32),
            pltpu.VMEM((rows_per_step, DK_W), BF16),
            pltpu.VMEM((rows_per_step, DV_W), BF16),
            pltpu.VMEM((rows_per_step, DK_W), BF16),
            pltpu.VMEM((rows_per_step, DV_W), BF16),
            pltpu.VMEM((past_len, LANES), BF16),
            pltpu.VMEM((_QROWS * SEL_BLOCK, LANES), BF16),
            pltpu.VMEM((past_len // CMP_BLOCK, LANES), F32),
            pltpu.VMEM((_QROWS, 1), F32),
            pltpu.VMEM((_QROWS, 1), F32),
            pltpu.VMEM((_QROWS, DV_W), F32),
        ],
    )
    return pl.pallas_call(
        functools.partial(_decode_kernel, n_chunks=n_chunks, lam_init=lam_init),
        grid_spec=grid_spec,
        out_shape=[jax.ShapeDtypeStruct((nb, 1, NSA_W), BF16),
                   jax.ShapeDtypeStruct((nb, 1, QD_W), BF16)],
        compiler_params=_cparams(2),
        name="decode",
    )(page_table, *([cache] * npg), *seq_ins, *const_ins)


def _merge_kernel(oa_ref, ob_ref, gm_ref, x_ref, ga_ref, wa_ref, wb_ref, wo_ref, o_ref):
    d = x_ref.shape[-1]
    ya = _dot(oa_ref[...], wa_ref[...])
    yb = _dot(ob_ref[...], wb_ref[...])
    mg = gm_ref[:, 0:d] * ya + gm_ref[:, d:2 * d] * yb
    o_ref[...] = x_ref[...] + ga_ref[0] * _dot(mg.astype(BF16), wo_ref[...])


def _merge(oa, ob, gm, x2d, ga, tm, p):
    n, d = x2d.shape
    steps = n // tm
    r = ga.shape[1]
    steps_per_group = steps // ga.shape[0]

    def rows(width):
        return pl.BlockSpec((tm, width), lambda i: (i, 0))

    def const(a):
        return pl.BlockSpec(a.shape, lambda i: (0,) * a.ndim)

    return pl.pallas_call(
        _merge_kernel,
        grid=(steps,),
        in_specs=[rows(NSA_W), rows(QD_W), rows(2 * d), rows(d),
                  pl.BlockSpec((1, r, d), lambda i: (i // steps_per_group, 0, 0)),
                  const(p["w_br_nsa"]), const(p["w_br_diff"]), const(p["w_out"])],
        out_specs=rows(d),
        out_shape=jax.ShapeDtypeStruct((n, d), F32),
        compiler_params=_cparams(1),
        name="merge",
    )(oa, ob, gm, x2d, ga, p["w_br_nsa"], p["w_br_diff"], p["w_out"])


def _ffn_kernel(x_ref, sh_ref, sc_ref, ga_ref, g2_ref, wi_ref, wo_ref, o_ref):
    x = x_ref[...]
    d_ff = wo_ref.shape[0]
    ms = jnp.mean(x * x, axis=-1, keepdims=True)
    h = x * lax.rsqrt(ms + EPS) * g2_ref[...]
    hb = (h * (1.0 + sc_ref[0]) + sh_ref[0]).astype(BF16)
    u = _dot(hb, wi_ref[...])
    ug = u[:, 0:d_ff]
    act = (ug * jax.nn.sigmoid(ug) * u[:, d_ff:]).astype(BF16)
    o_ref[...] = x + ga_ref[0] * _dot(act, wo_ref[...])


def _ffn(x2d, sh, sc, ga, tm, p):
    n, d = x2d.shape
    steps = n // tm
    r = sh.shape[1]
    steps_per_group = steps // sh.shape[0]
    rows = pl.BlockSpec((tm, d), lambda i: (i, 0))
    mod_spec = pl.BlockSpec((1, r, d), lambda i: (i // steps_per_group, 0, 0))

    def const(a):
        return pl.BlockSpec(a.shape, lambda i: (0,) * a.ndim, pipeline_mode=pl.Buffered(1))

    return pl.pallas_call(
        _ffn_kernel,
        grid=(steps,),
        in_specs=[rows, mod_spec, mod_spec, mod_spec, const(p["g2"]), const(p["w_ffn_in"]),
                  const(p["w_ffn_out"])],
        out_specs=rows,
        out_shape=jax.ShapeDtypeStruct((n, d), F32),
        compiler_params=_cparams(1),
        name="ffn",
    )(x2d, sh, sc, ga, p["g2"], p["w_ffn_in"], p["w_ffn_out"])


def _tiled_gain(g, reps, scale=1.0):
    return (jnp.tile(g.astype(F32), reps) * scale).reshape(1, -1)


def _slope_lanes(slopes):
    out = jnp.zeros((len(slopes), LANES), F32)
    rest = jnp.asarray(slopes, F32) * LOG2E
    for t in range(SLOPE_TERMS):
        term = rest.astype(BF16).astype(F32)
        rest = rest - term
        out = out.at[:, HEAD_DIM + 2 * t].set(term * POS_SPLIT).at[:, HEAD_DIM + 2 * t + 1].set(term)
    return out.reshape(1, -1)


def _half_gain(g):
    return jnp.concatenate([g.astype(F32), jnp.ones((HEAD_DIM,), F32)]).reshape(1, -1)


def _compress_weights(wk, wv, rows):
    col = jnp.concatenate([jnp.tile(wk.astype(F32)[:, None], (1, HEAD_DIM)),
                           jnp.tile(wv.astype(F32)[:, None], (1, HEAD_DIM))], axis=1)
    return jnp.tile(col, (rows // CMP_BLOCK, 1))


def _prepare_params(l, tm_prompt, w_ada, b_ada, norm1_g, norm2_g, w_in, nsa_q_norm, nsa_k_norm,
                    nsa_cmp_wk, nsa_cmp_wv, nsa_cmp_pe, diff_q_norm, diff_k_norm, diff_lambda,
                    diff_subln, w_br_nsa, w_br_diff, w_out, w_ffn_in, w_ffn_out):
    d = w_in.shape[1]
    wi = w_in[l]
    o_kv = NSA_W
    o_win = o_kv + KV_W
    o_gl = o_win + WIN_W
    o_qd = o_gl + GL_W
    o_mg = o_qd + QD_W
    w_r = jnp.concatenate([
        wi[:, 0:NSA_W], wi[:, o_qd:o_mg], wi[:, o_kv:o_win], wi[:, o_win:o_gl],
        wi[:, o_mg:o_mg + 2 * d], wi[:, o_gl:o_qd],
        jnp.zeros((d, LANES - GL_W), wi.dtype)], axis=1).astype(BF16)
    pe2 = jnp.concatenate([nsa_cmp_pe[l].astype(F32), jnp.zeros((CMP_BLOCK, HEAD_DIM), F32)], axis=1)
    dec_rows = _PAGES_PER_STEP * PAGE_SIZE
    nsa_slopes = _alibi_slopes(NSA_HEADS)
    diff_slopes = _alibi_slopes(DIFF_KV_HEADS * DIFF_GROUP)
    g64 = jnp.arange(2 * LANES) // HEAD_DIM
    return {
        "w_ada": w_ada[l], "b_ada": b_ada[l],
        "g1": norm1_g[l].astype(F32).reshape(1, -1), "g2": norm2_g[l].astype(F32).reshape(1, -1),
        "w_in": w_r,
        "gmat64": ((g64[:, None] == g64[None, :]) * (1.0 / HEAD_DIM)).astype(BF16),
        "qg": _tiled_gain(nsa_q_norm[l], NSA_HEADS, QK_SCALE),
        "qg_log2": _tiled_gain(nsa_q_norm[l], NSA_HEADS, QK_SCALE * LOG2E),
        "qaug": _slope_lanes(nsa_slopes),
        "dqg": _tiled_gain(diff_q_norm[l], N_QD, QK_SCALE),
        "dqg_log2": _tiled_gain(diff_q_norm[l], N_QD, QK_SCALE * LOG2E),
        "dqaug": _slope_lanes([diff_slopes[j // 2] for j in range(N_QD)]),
        "kcg": _half_gain(nsa_k_norm[l, 0]),
        "selg64": _half_gain(nsa_k_norm[l, 1]), "wing64": _half_gain(nsa_k_norm[l, 2]),
        "dkg64": _tiled_gain(diff_k_norm[l], N_KD),
        "wcol": _compress_weights(nsa_cmp_wk[l], nsa_cmp_wv[l], tm_prompt),
        "pe": jnp.tile(pe2, (tm_prompt // CMP_BLOCK, 1)),
        "wcol_dec": _compress_weights(nsa_cmp_wk[l], nsa_cmp_wv[l], dec_rows),
        "pe_dec": jnp.tile(pe2, (dec_rows // CMP_BLOCK, 1)),
        "lam": diff_lambda[l].astype(F32), "subln": diff_subln[l].astype(F32).reshape(1, -1),
        "w_br_nsa": w_br_nsa[l].astype(BF16), "w_br_diff": w_br_diff[l].astype(BF16),
        "w_out": w_out[l].astype(BF16),
        "w_ffn_in": w_ffn_in[l].astype(BF16), "w_ffn_out": w_ffn_out[l].astype(BF16),
    }


def _pad_rows(a, rows):
    return jnp.concatenate([a, jnp.zeros((a.shape[0], rows - a.shape[1]) + a.shape[2:], a.dtype)], axis=1)


def _unblock(a, n):
    return a.reshape(a.shape[0], n, LANES)[:, :, :HEAD_DIM]


def _decode_queries(qn, qd):
    nb = qn.shape[0]
    qn16 = _pad_rows(qn, _QROWS)
    q = qd.reshape(nb, DIFF_KV_HEADS, DIFF_GROUP, 2, HEAD_DIM)
    slot = jnp.arange(N_KD).reshape(DIFF_KV_HEADS, 1, 2)
    onehot = (slot[..., None] == jnp.arange(N_KD)).astype(q.dtype)
    qmat = q[:, :, :, :, None, :] * onehot[None, :, :, :, :, None]
    qd16 = _pad_rows(qmat.reshape(nb, N_QD, DK_W), _QROWS)
    return qn16, qd16


def _layer(l, xp, xs, cache_kv, state_win_kv, page_table, mod, p, tm, tiles):
    batch, seq, d = xp.shape
    nb = xs.shape[0]
    win_l = state_win_kv[l]
    lam_init = 0.8 - 0.6 * math.exp(-0.3 * l)
    mod_p = mod[:batch].reshape(batch, 1, 6 * d)
    mod_s = mod[batch:batch + nb].reshape(1, nb, 6 * d)

    def part(m, k):
        return m[:, :, k * d:(k + 1) * d]

    x2 = xp.reshape(batch * seq, d)
    (kv, win, qn, qd, ksel, kwin, vnsa, kd, vd, gl, gm, kcvc) = _inproj(
        x2, part(mod_p, 0), part(mod_p, 1), tm, p, seq, compress=True)
    oa = _nsa_prompt(qn, kcvc, ksel, kwin, vnsa, gl, batch, seq, tiles["nsa_tq"], tiles["nsa_tk"])
    ob = _diff_prompt(qd, kd, vd, p["lam"], p["subln"], batch, seq, tiles["diff_tq"],
                      tiles["diff_tk"], lam_init)
    x1 = _merge(oa, ob, gm, x2, part(mod_p, 2), tm, p)
    yp = _ffn(x1, part(mod_p, 3), part(mod_p, 4), part(mod_p, 5), tm, p)
    w_buf = win_l.shape[1]
    win_p = win.reshape(batch, seq, WIN_W)[:, seq - w_buf:]

    xs2 = xs.reshape(nb, d)
    (kv_s, win_s, qn_s, qd_s, ksel_s, kwin_s, vnsa_s, kd_s, vd_s, gl_s, gm_s) = _inproj(
        xs2, part(mod_s, 0), part(mod_s, 1), nb, p, 1, compress=False)
    qn16, qd16 = _decode_queries(_unblock(qn_s, NSA_HEADS), _unblock(qd_s, N_QD))
    selkv_s = jnp.concatenate([ksel_s[:, :HEAD_DIM], vnsa_s[:, :HEAD_DIM]], axis=1)
    winkv_s = jnp.concatenate([kwin_s[:, :HEAD_DIM], vnsa_s[:, HEAD_DIM:]], axis=1)
    kd_plain = _unblock(kd_s, N_KD).reshape(nb, DK_W)
    oa_s, ob_s = _decode(l, page_table, cache_kv, qn16, qd16, selkv_s[:, None], winkv_s[:, None],
                         kd_plain[:, None], vd_s[:, None], gl_s[:, None], state_win_kv, p, lam_init)
    x1s = _merge(oa_s.reshape(nb, NSA_W), ob_s.reshape(nb, QD_W), gm_s, xs2, part(mod_s, 2), nb, p)
    ys = _ffn(x1s, part(mod_s, 3), part(mod_s, 4), part(mod_s, 5), nb, p)
    win_state_s = jnp.concatenate([win_l[:, 1:], win_s[:, None]], axis=1)
    return (yp.reshape(batch, seq, d), ys.reshape(nb, 1, d), kv.reshape(batch, seq, KV_W),
            kv_s.reshape(nb, 1, KV_W), win_p, win_state_s)


def _tiles(seq):
    return {"nsa_tq": min(256, seq), "nsa_tk": min(512, seq), "diff_tq": min(256, seq),
            "diff_tk": min(1024, seq)}


def kernel(x_prompt, x_sample, cache_kv, state_win_kv, page_table, c_prompt, c_sample, w_ada, b_ada,
           norm1_g, norm2_g, w_in, nsa_q_norm, nsa_k_norm, nsa_cmp_wk, nsa_cmp_wv, nsa_cmp_pe,
           diff_q_norm, diff_k_norm, diff_lambda, diff_subln, w_br_nsa, w_br_diff, w_out, w_ffn_in,
           w_ffn_out):
    depth = w_in.shape[0]
    seq = x_prompt.shape[1]
    assert x_sample.shape[1] == 1 and cache_kv.shape[2] == PAGE_SIZE
    tm = min(512, seq)
    xp, xs = x_prompt, x_sample
    kv_p, kv_s, win_p, win_s = [], [], [], []
    n_cond = c_prompt.shape[0] + c_sample.shape[0]
    c_all = jnp.concatenate([c_prompt, c_sample,
                             jnp.zeros((-n_cond % 16, c_prompt.shape[1]), c_prompt.dtype)], axis=0)
    for l in range(depth):
        p = _prepare_params(l, tm, w_ada, b_ada, norm1_g, norm2_g, w_in, nsa_q_norm, nsa_k_norm,
                            nsa_cmp_wk, nsa_cmp_wv, nsa_cmp_pe, diff_q_norm, diff_k_norm,
                            diff_lambda, diff_subln, w_br_nsa, w_br_diff, w_out, w_ffn_in, w_ffn_out)
        mod = _ada(c_all, p["w_ada"], p["b_ada"])
        xp, xs, kvp, kvs, wp, ws = _layer(l, xp, xs, cache_kv, state_win_kv, page_table, mod, p, tm,
                                          _tiles(seq))
        kv_p.append(kvp)
        kv_s.append(kvs)
        win_p.append(wp)
        win_s.append(ws)
    return (xp, xs, jnp.stack(kv_p), jnp.stack(kv_s), jnp.stack(win_p), jnp.stack(win_s))
```

```python
import functools
import math

import jax
import jax.numpy as jnp
from jax import lax
from jax.experimental import pallas as pl
from jax.experimental.pallas import tpu as pltpu

F32 = jnp.float32
BF16 = jnp.bfloat16

HEAD_DIM = 64
NSA_HEADS = 8
CMP_BLOCK = 32
SEL_BLOCK = 64
TOP_N = 16
WINDOW = 512
DIFF_KV_HEADS = 2
DIFF_GROUP = 2
DIFF_VDIM = 2 * HEAD_DIM
PAGE_SIZE = 128
EPS = 1e-6
NEG_INF = -1e30
MASK_BIAS = -(2.0 ** 100)
FORCE_SCORE = 1e9
QK_SCALE = HEAD_DIM ** -0.5

LANES = 128
POS_SPLIT = 256
SLOPE_TERMS = 3
LOG2E = math.log2(math.e)
NSA_W = NSA_HEADS * HEAD_DIM
KV_W = 4 * HEAD_DIM + 2 * DIFF_KV_HEADS * 2 * HEAD_DIM
WIN_W = 2 * HEAD_DIM
GL_W = 3 * NSA_HEADS
N_QD = DIFF_KV_HEADS * DIFF_GROUP * 2
N_KD = DIFF_KV_HEADS * 2
QD_W = N_QD * HEAD_DIM
DK_W = N_KD * HEAD_DIM
DV_W = DIFF_KV_HEADS * DIFF_VDIM

VMEM_LIMIT_BYTES = 56 * 1024 * 1024


def _cparams(n_grid_dims):
    return pltpu.CompilerParams(
        dimension_semantics=("arbitrary",) * n_grid_dims,
        vmem_limit_bytes=VMEM_LIMIT_BYTES,
    )


def _dot(a, b):
    return jnp.dot(a, b, preferred_element_type=F32)


def _dot_nt(a, b):
    return lax.dot_general(a, b, (((1,), (1,)), ((), ())), preferred_element_type=F32)


def _split_dot(x, mat):
    hi = x.astype(BF16)
    lo = (x - hi.astype(F32)).astype(BF16)
    return _dot(hi, mat) + _dot(lo, mat)


def _group_sumsq(x, gmat):
    return _split_dot(x * x, gmat)


def _block_sumsq(x, gmat):
    w = x.shape[1]
    if w <= gmat.shape[0]:
        return _group_sumsq(x, gmat[0:w, 0:w])
    cw = gmat.shape[0]
    return jnp.concatenate([_group_sumsq(x[:, c:c + cw], gmat) for c in range(0, w, cw)], axis=1)


def _lane_iota(shape):
    return lax.broadcasted_iota(jnp.int32, shape, len(shape) - 1)


def _row_iota(shape):
    return lax.broadcasted_iota(jnp.int32, shape, 0)


def _alibi_slopes(n):
    return [2.0 ** (-8.0 * (i + 1) / n) for i in range(n)]


def _row_const(rows_per_group, values):
    n = len(values)
    row = _row_iota((n * rows_per_group, 1))
    out = jnp.full((n * rows_per_group, 1), values[0], F32)
    for g in range(1, n):
        out = jnp.where(row >= g * rows_per_group, values[g], out)
    return out


def _pos_lanes(pos, width):
    lane = (_lane_iota((pos.shape[0], width)) & (LANES - 1)) - HEAD_DIM
    hi = (pos >> 8).astype(F32)
    lo = (pos & (POS_SPLIT - 1)).astype(F32)
    in_range = (lane >= 0) & (lane < 2 * SLOPE_TERMS)
    return jnp.where(in_range, jnp.where((lane & 1) == 0, hi, lo), 0.0)


def _online_step(s, v, m, l, acc):
    m_new = jnp.maximum(m, jnp.max(s, axis=-1, keepdims=True))
    alpha = jnp.exp(m - m_new)
    p = jnp.exp(s - m_new)
    l = alpha * l + jnp.sum(p, axis=-1, keepdims=True)
    acc = alpha * acc + _dot(p.astype(BF16), v)
    return m_new, l, acc


def _flash_init(m_s, l_s, acc_s):
    m_s[...] = jnp.full(m_s.shape, NEG_INF, F32)
    l_s[...] = jnp.zeros(l_s.shape, F32)
    acc_s[...] = jnp.zeros(acc_s.shape, F32)


def _flash_update(load_col, n_cols, v, m_s, l_s, acc_s, r0, nrows):
    sl = slice(r0, r0 + nrows)
    smax = load_col(0)
    for c in range(1, n_cols):
        smax = jnp.maximum(smax, load_col(c))
    m_old = m_s[sl]
    m_new = jnp.maximum(m_old, jnp.max(smax, axis=-1, keepdims=True))
    alpha = jnp.exp2(m_old - m_new)
    acc_s[sl] = alpha * acc_s[sl]
    psum = alpha * l_s[sl]
    p_bf = []
    for c in range(n_cols):
        p = jnp.exp2(load_col(c) - m_new)
        psum = psum + p
        p_bf.append(p.astype(BF16))
    l_s[sl] = psum
    m_s[sl] = m_new
    acc_s[sl] += _dot(jnp.concatenate(p_bf, axis=1), v)


def _score_loader(s_buf, slot, r0, nrows, k0, q0, tq, causal):
    def load_col(c):
        s = s_buf[slot, r0:r0 + nrows, c * LANES:(c + 1) * LANES]
        if causal:
            qp = q0 + ((r0 + _row_iota((nrows, 1))) & (tq - 1))
            s = jnp.where(k0 + c * LANES + _lane_iota((1, LANES)) <= qp, s, NEG_INF)
        return s
    return load_col


def _flash_finish(m_s, l_s, acc_s):
    l = jnp.sum(l_s[...], axis=-1, keepdims=True)
    return jnp.where(m_s[...] > 0.5 * NEG_INF, acc_s[...] / jnp.maximum(l, 1e-30), 0.0)


def _flash_loop(qk, consume, last):
    qk(0, 0)

    def body(j, carry):
        kt = 2 * j
        qk(kt + 1, 1)
        consume(kt, 0, False)
        qk(kt + 2, 0)
        consume(kt + 1, 1, False)
        return carry

    lax.fori_loop(0, last >> 1, body, 0)

    @pl.when((last & 1) == 1)
    def _():
        qk(last, 1)
        consume(last - 1, 0, False)
        consume(last, 1, True)

    @pl.when((last & 1) == 0)
    def _():
        consume(last, 0, True)


def _finish(m, l, acc):
    return jnp.where(m > 0.5 * NEG_INF, acc / jnp.maximum(l, 1e-30), 0.0)


def _compress(x, wcol):
    r = x.shape[0]
    return jnp.sum((x * wcol).reshape(r // CMP_BLOCK, CMP_BLOCK, x.shape[1]), axis=1)


def _norm_compressed(raw, kcg):
    is_k = _lane_iota(raw.shape) < HEAD_DIM
    ss = jnp.sum(jnp.where(is_k, raw * raw, 0.0), axis=-1, keepdims=True)
    return jnp.where(is_k, raw * lax.rsqrt(ss * (1.0 / HEAD_DIM) + EPS) * kcg, raw)


def _ada_kernel(c_ref, w_ref, b_ref, o_ref):
    c = c_ref[...]
    sc = c * jax.nn.sigmoid(c)
    o_ref[...] = _dot(sc.astype(BF16), w_ref[...].astype(BF16)) + b_ref[...]


def _ada(c, w_ada, b_ada):
    m, d = c.shape
    n = w_ada.shape[1]
    tn = d
    return pl.pallas_call(
        _ada_kernel,
        grid=(n // tn,),
        in_specs=[
            pl.BlockSpec((m, d), lambda j: (0, 0)),
            pl.BlockSpec((d, tn), lambda j: (0, j)),
            pl.BlockSpec((1, tn), lambda j: (0, j)),
        ],
        out_specs=pl.BlockSpec((m, tn), lambda j: (0, j)),
        out_shape=jax.ShapeDtypeStruct((m, n), F32),
        compiler_params=_cparams(1),
        name="ada",
    )(c, w_ada, b_ada.reshape(1, n))


_C_QN = 0
_C_QD = _C_QN + NSA_W
_C_KV = _C_QD + QD_W
_C_WIN = _C_KV + KV_W
_C_MG = _C_WIN + WIN_W


def _inproj_kernel(*refs, compress, steps_per_seq):
    (x_ref, sh_ref, sc_ref, g1_ref, w_ref, gmat_ref, qg_ref, qaug_ref, dqg_ref, dqaug_ref, selg_ref,
     wing_ref, dkg_ref) = refs[:13]
    n_in = 13
    if compress:
        wcol_ref, pe_ref, kcg_ref = refs[13:16]
        n_in = 16
    (kv_ref, win_ref, qn_ref, qd_ref, ksel_ref, kwin_ref, vnsa_ref, kd_ref, vd_ref, gl_ref,
     gm_ref) = refs[n_in:n_in + 11]
    tm, d_model = x_ref.shape
    c_gl = _C_MG + 2 * d_model

    x = x_ref[...]
    ms = jnp.mean(x * x, axis=-1, keepdims=True)
    h = x * lax.rsqrt(ms + EPS) * g1_ref[...]
    h = h * (1.0 + sc_ref[0]) + sh_ref[0]
    hb = h.astype(BF16)
    low = _lane_iota((tm, LANES)) < HEAD_DIM

    def proj(a, b):
        return _dot(hb, w_ref[:, a:b])

    def normed(z, gain):
        return z * lax.rsqrt(_block_sumsq(z, gmat_ref[...]) + EPS) * gain

    def spread(z):
        blocks = []
        for j in range(0, z.shape[1], LANES):
            pair = z[:, j:j + LANES]
            blocks.append(jnp.where(low, pair, 0.0))
            blocks.append(jnp.where(low, pltpu.roll(pair, HEAD_DIM, axis=1), 0.0))
        return jnp.concatenate(blocks, axis=1)

    def k_half(piece, gain):
        ms_k = _group_sumsq(piece, gmat_ref[0:LANES, 0:LANES])
        return jnp.where(low, piece * lax.rsqrt(ms_k + EPS) * gain, 0.0)

    qn_ref[...] = (spread(normed(proj(_C_QN, _C_QD), qg_ref[...])) + qaug_ref[...]).astype(BF16)
    qd_ref[...] = (spread(normed(proj(_C_QD, _C_KV), dqg_ref[...])) + dqaug_ref[...]).astype(BF16)

    pos = (pl.program_id(0) % steps_per_seq) * tm + _row_iota((tm, 1))
    kv = proj(_C_KV, _C_WIN)
    kv_ref[...] = kv
    win = proj(_C_WIN, _C_MG)
    win_ref[...] = win
    sel_piece = kv[:, 2 * HEAD_DIM:4 * HEAD_DIM]
    ksel = k_half(sel_piece, selg_ref[...]) + _pos_lanes(pos, LANES)
    block_onehot = jnp.where(_lane_iota((tm, LANES)) == (pos >> 6), 1.0, 0.0)
    ksel_ref[...] = jnp.concatenate([ksel, block_onehot], axis=1).astype(BF16)
    kwin_ref[...] = (k_half(win, wing_ref[...]) + _pos_lanes(pos, LANES)).astype(BF16)
    dk = kv[:, 4 * HEAD_DIM:4 * HEAD_DIM + DK_W]
    kd_ref[...] = (spread(normed(dk, dkg_ref[...])) + _pos_lanes(pos, N_KD * LANES)).astype(BF16)
    vd_ref[...] = kv[:, 4 * HEAD_DIM + DK_W:].astype(BF16)
    sel_v_first = pltpu.roll(sel_piece, HEAD_DIM, axis=1)
    vnsa_ref[...] = jnp.where(low, sel_v_first, win).astype(BF16)

    gm_ref[...] = jax.nn.sigmoid(proj(_C_MG, c_gl)).astype(gm_ref.dtype)
    gl_ref[...] = jax.nn.sigmoid(proj(c_gl, c_gl + LANES))

    if compress:
        kcvc_ref = refs[n_in + 11]
        raw = _compress(kv[:, 0:2 * HEAD_DIM] + pe_ref[...], wcol_ref[...])
        kcvc_ref[...] = _norm_compressed(raw, kcg_ref[...])


def _inproj(x2d, sh, sc, tm, p, seq, compress):
    n, d = x2d.shape
    steps = n // tm
    r = sh.shape[1]
    steps_per_group = steps // sh.shape[0]
    w = p["w_in"]

    def const(a):
        return pl.BlockSpec(a.shape, lambda i: (0,) * a.ndim, pipeline_mode=pl.Buffered(1))

    def rows(width):
        return pl.BlockSpec((tm, width), lambda i: (i, 0))

    mod_spec = pl.BlockSpec((1, r, d), lambda i: (i // steps_per_group, 0, 0))
    qg, dqg = (p["qg_log2"], p["dqg_log2"]) if compress else (p["qg"], p["dqg"])
    consts = [p["g1"], w, p["gmat64"], qg, p["qaug"], dqg, p["dqaug"], p["selg64"],
              p["wing64"], p["dkg64"]]
    ins = [x2d, sh, sc] + consts
    in_specs = [rows(d), mod_spec, mod_spec] + [const(a) for a in consts]
    outs = [(KV_W, F32), (WIN_W, F32), (NSA_HEADS * LANES, BF16), (N_QD * LANES, BF16),
            (2 * LANES, BF16), (LANES, BF16), (LANES, BF16), (N_KD * LANES, BF16), (DV_W, BF16),
            (LANES, F32), (2 * d, BF16)]
    out_shape = [jax.ShapeDtypeStruct((n, wd), dt) for wd, dt in outs]
    out_specs = [rows(wd) for wd, _ in outs]
    if compress:
        cins = [p["wcol"], p["pe"], p["kcg"]]
        ins += cins
        in_specs += [const(a) for a in cins]
        out_shape.append(jax.ShapeDtypeStruct((n // CMP_BLOCK, LANES), F32))
        out_specs.append(pl.BlockSpec((tm // CMP_BLOCK, LANES), lambda i: (i, 0)))
    return pl.pallas_call(
        functools.partial(_inproj_kernel, compress=compress, steps_per_seq=max(seq // tm, 1)),
        grid=(steps,),
        in_specs=in_specs,
        out_specs=out_specs,
        out_shape=out_shape,
        compiler_params=_cparams(1),
        name="inproj_cmp" if compress else "inproj",
    )(*ins)


def _stack_blocks(q, n):
    return jnp.concatenate([q[:, h * LANES:(h + 1) * LANES] for h in range(n)], axis=0)


def _pair_sum(x, n_out):
    n2 = x.shape[1]
    r = _row_iota((n2, n_out))
    c = _lane_iota((n2, n_out))
    pmat = jnp.where((r >> 1) == c, 1.0, 0.0).astype(BF16)
    hi = x.astype(BF16)
    r1 = x - hi.astype(F32)
    mid = r1.astype(BF16)
    lo = (r1 - mid.astype(F32)).astype(BF16)
    return _dot(hi, pmat) + _dot(mid, pmat) + _dot(lo, pmat)


def _top_blocks(score, n_top):
    work = score.T
    idx = _row_iota(work.shape).astype(F32)
    big = float(work.shape[0])
    sel = jnp.zeros_like(work)
    for _ in range(n_top):
        mx = jnp.max(work, axis=0, keepdims=True)
        first = jnp.min(jnp.where(work == mx, idx, big), axis=0, keepdims=True)
        hit = idx == first
        work, sel = jnp.where(hit, -2.0, work), jnp.where(hit, 1.0, sel)
    return sel.T


def _rank_select(score_row, n_top):
    n = score_row.shape[1]
    r = jnp.broadcast_to(score_row, (n, n))
    c = r.T
    i = _row_iota((n, n))
    j = _lane_iota((n, n))
    beats = (c > r) | ((c == r) & (i < j))
    rank = jnp.sum(jnp.where(beats, 1.0, 0.0), axis=0, keepdims=True)
    return rank < n_top


_ROW_CHUNK = 256


def _nsa_kernel(qn_ref, kcvc_ref, ksel_ref, kwin_ref, v_ref, gate_ref, o_ref, m_s, l_s, acc_s,
                s_buf, *, tq, tk, n_sel):
    qi = pl.program_id(1)
    nh = NSA_HEADS
    rows = nh * tq
    q0 = qi * tq
    qs = _stack_blocks(qn_ref[...], nh)
    qpos_t = q0 + _row_iota((tq, 1))

    def head(a, h):
        return a[h * tq:(h + 1) * tq]

    kcvc = kcvc_ref[...]
    n_cmp = kcvc.shape[0]
    cmp_end = _row_iota((n_cmp, 1)) * CMP_BLOCK + (CMP_BLOCK - 1)
    kc = jnp.where(_lane_iota(kcvc.shape) < HEAD_DIM, kcvc, _pos_lanes(cmp_end, LANES)).astype(BF16)
    ok_c = (_lane_iota((1, n_cmp)) * CMP_BLOCK + (CMP_BLOCK - 1)) <= qpos_t
    s_all = _dot_nt(qs, kc)
    p_heads = jnp.zeros((tq, n_cmp), F32)
    p_bf = []
    for h in range(nh):
        s = jnp.where(ok_c, head(s_all, h), NEG_INF)
        e = jnp.where(ok_c, jnp.exp2(s - jnp.max(s, axis=-1, keepdims=True)), 0.0)
        p = e / jnp.maximum(jnp.sum(e, axis=-1, keepdims=True), 1e-30)
        p_heads = p_heads + p
        p_bf.append(p.astype(BF16))
    o_c = _dot(jnp.concatenate(p_bf, axis=0), kcvc.astype(BF16))
    imp = _pair_sum(p_heads, LANES)
    blk = _lane_iota((1, LANES))
    forced = (blk == (qpos_t >> 6)) | (blk == 0)
    valid = blk * SEL_BLOCK <= qpos_t
    score = jnp.where(valid, jnp.where(forced, FORCE_SCORE, imp), -1.0)

    span = tq + WINDOW
    start = pl.multiple_of(jnp.maximum(q0 - WINDOW, 0), tq)
    kw = kwin_ref[pl.ds(start, span), :]
    vw = v_ref[pl.ds(start, span), :]
    dist = qpos_t - (start + _lane_iota((1, span)))
    ok_w = (dist >= 0) & (dist < WINDOW)
    s_all = _dot_nt(qs, kw)
    p_bf, l_w = [], []
    for h in range(nh):
        s = jnp.where(ok_w, head(s_all, h), NEG_INF)
        e = jnp.exp2(s - jnp.max(s, axis=-1, keepdims=True))
        l_w.append(jnp.sum(e, axis=-1, keepdims=True))
        p_bf.append(e.astype(BF16))
    o_w = _dot(jnp.concatenate(p_bf, axis=0), vw)
    gate = gate_ref[...]
    upper = [gate[:, 3 * h:3 * h + 1] * head(o_c, h)
             + gate[:, 3 * h + 2:3 * h + 3] * (head(o_w, h) / l_w[h]) for h in range(nh)]

    sel = _top_blocks(score, min(TOP_N, n_sel))
    desel = jnp.where(valid & (sel > 0.5), 0.0, MASK_BIAS).astype(BF16)
    qfull = jnp.concatenate([qs, jnp.concatenate([desel] * nh, axis=0)], axis=1)

    _flash_init(m_s, l_s, acc_s)
    rc = min(_ROW_CHUNK, rows)

    def sel_qk(kt, slot):
        k = ksel_ref[pl.ds(pl.multiple_of(kt * tk, tk), tk), :]
        s_buf[slot] = _dot_nt(qfull, k)

    def sel_consume(kt, slot, causal):
        v = v_ref[pl.ds(pl.multiple_of(kt * tk, tk), tk), :]
        for r0 in range(0, rows, rc):
            _flash_update(_score_loader(s_buf, slot, r0, rc, kt * tk, q0, tq, causal), tk // LANES,
                          v, m_s, l_s, acc_s, r0, rc)

    _flash_loop(sel_qk, sel_consume, q0 // tk)
    o_s = _flash_finish(m_s, l_s, acc_s)

    low = _lane_iota((tq, LANES)) < HEAD_DIM
    outs = []
    for h in range(nh):
        both = jnp.where(low, gate[:, 3 * h + 1:3 * h + 2] * head(o_s, h), upper[h])
        outs.append(both + pltpu.roll(both, HEAD_DIM, axis=1))
    for j in range(nh // 2):
        o_ref[:, j * LANES:(j + 1) * LANES] = jnp.where(low, outs[2 * j], outs[2 * j + 1]).astype(o_ref.dtype)


def _nsa_prompt(qn, kcvc, ksel, kwin, vnsa, gates, batch, seq, tq, tk):
    nq = seq // tq
    n_cmp = seq // CMP_BLOCK
    n_sel = seq // SEL_BLOCK
    assert n_sel <= LANES and seq % tk == 0 and tk % tq == 0 and seq >= tq + WINDOW
    rows = NSA_HEADS * tq

    def qrow(width):
        return pl.BlockSpec((tq, width), lambda b, i: (b * nq + i, 0))

    def per_batch(rows_, width):
        return pl.BlockSpec((rows_, width), lambda b, i: (b, 0), pipeline_mode=pl.Buffered(1))

    return pl.pallas_call(
        functools.partial(_nsa_kernel, tq=tq, tk=tk, n_sel=n_sel),
        grid=(batch, nq),
        in_specs=[qrow(NSA_HEADS * LANES), per_batch(n_cmp, LANES), per_batch(seq, 2 * LANES),
                  per_batch(seq, LANES), per_batch(seq, LANES), qrow(LANES)],
        out_specs=qrow(NSA_W),
        out_shape=jax.ShapeDtypeStruct((batch * seq, NSA_W), BF16),
        scratch_shapes=[pltpu.VMEM((rows, LANES), F32), pltpu.VMEM((rows, LANES), F32),
                        pltpu.VMEM((rows, LANES), F32), pltpu.VMEM((2, rows, tk), F32)],
        compiler_params=_cparams(2),
        name="nsa_prompt",
    )(qn, kcvc, ksel, kwin, vnsa, gates)


def _lambda_full(lam_ref, lam_init):
    dl = lam_ref[...]
    a = jnp.sum(dl[0:1] * dl[1:2], axis=-1, keepdims=True)
    b = jnp.sum(dl[2:3] * dl[3:4], axis=-1, keepdims=True)
    return jnp.exp(a) - jnp.exp(b) + lam_init


def _subln(o, gain, lam_init):
    ms = jnp.mean(o * o, axis=-1, keepdims=True)
    return o * lax.rsqrt(ms + EPS) * gain * (1.0 - lam_init)


def _diff_kernel(q_ref, k_ref, v_ref, lam_ref, subln_ref, o_ref, m_s, l_s, acc_s, s_buf, *, tq, tk,
                 lam_init):
    qi = pl.program_id(2)
    g = DIFF_GROUP
    half = g * tq
    rows = 2 * half
    q0 = qi * tq
    q = q_ref[...]
    qm = [jnp.concatenate([q[:, (gg * 2 + m) * LANES:(gg * 2 + m + 1) * LANES] for gg in range(g)],
                          axis=0) for m in range(2)]
    _flash_init(m_s, l_s, acc_s)
    rc = min(_ROW_CHUNK, half)

    def qk(kt, slot):
        k = k_ref[pl.ds(pl.multiple_of(kt * tk, tk), tk), :]
        for m in range(2):
            s_buf[slot, m * half:(m + 1) * half, :] = _dot_nt(qm[m], k[:, m * LANES:(m + 1) * LANES])

    def consume(kt, slot, causal):
        v = v_ref[pl.ds(pl.multiple_of(kt * tk, tk), tk), :]
        for r0 in range(0, rows, rc):
            _flash_update(_score_loader(s_buf, slot, r0, rc, kt * tk, q0, tq, causal), tk // LANES,
                          v, m_s, l_s, acc_s, r0, rc)

    _flash_loop(qk, consume, q0 // tk)
    o = _flash_finish(m_s, l_s, acc_s)
    lam = _lambda_full(lam_ref, lam_init)
    for gg in range(g):
        r0 = gg * tq
        d = o[r0:r0 + tq] - lam * o[half + r0:half + r0 + tq]
        o_ref[:, gg * DIFF_VDIM:(gg + 1) * DIFF_VDIM] = _subln(d, subln_ref[...], lam_init).astype(o_ref.dtype)


def _diff_prompt(qd, kd, vd, lam, subln, batch, seq, tq, tk, lam_init):
    nq = seq // tq
    assert seq % tk == 0 and tk % tq == 0
    rows = 2 * DIFF_GROUP * tq
    return pl.pallas_call(
        functools.partial(_diff_kernel, tq=tq, tk=tk, lam_init=lam_init),
        grid=(batch, DIFF_KV_HEADS, nq),
        in_specs=[
            pl.BlockSpec((tq, DIFF_GROUP * 2 * LANES), lambda b, n, i: (b * nq + i, n)),
            pl.BlockSpec((seq, 2 * LANES), lambda b, n, i: (b, n)),
            pl.BlockSpec((seq, DIFF_VDIM), lambda b, n, i: (b, n)),
            pl.BlockSpec(lam.shape, lambda b, n, i: (0, 0)),
            pl.BlockSpec(subln.shape, lambda b, n, i: (0, 0)),
        ],
        out_specs=pl.BlockSpec((tq, DIFF_GROUP * DIFF_VDIM), lambda b, n, i: (b * nq + i, n)),
        out_shape=jax.ShapeDtypeStruct((batch * seq, QD_W), BF16),
        scratch_shapes=[pltpu.VMEM((rows, LANES), F32), pltpu.VMEM((rows, LANES), F32),
                        pltpu.VMEM((rows, DIFF_VDIM), F32), pltpu.VMEM((2, rows, tk), F32)],
        compiler_params=_cparams(3),
        name="diff_prompt",
    )(qd, kd, vd, lam, subln)


_QROWS = 16
_PAGES_PER_STEP = 16


def _decode_kernel(pt_ref, *refs, n_chunks, lam_init):
    del pt_ref
    npg = _PAGES_PER_STEP
    pages = refs[:npg]
    (qn_ref, qd_ref, selkv_new_ref, winkv_new_ref, kd_new_ref, vd_new_ref, gate_ref, win_ref,
     gmat_ref, wcol_ref, pe_ref, kcg_ref, selg_ref, wing_ref, dkg_ref, lam_ref,
     subln_ref) = refs[npg:npg + 17]
    oa_ref, ob_ref = refs[npg + 17:npg + 19]
    (cmp_buf, kd_a, vd_a, kd_b, vd_b, stash, sel_buf, kcvc_s, dm_s, dl_s, dacc_s) = refs[npg + 19:]
    c = pl.program_id(1)
    rows_per_step = npg * PAGE_SIZE
    past_len = n_chunks * rows_per_step
    nh = NSA_HEADS

    slopes = _alibi_slopes(DIFF_KV_HEADS * DIFF_GROUP)
    dslope = _row_const(1, [slopes[r // 2] for r in range(8)] + [0.0] * (_QROWS - 8))
    qd = qd_ref[...]
    qd_gained = (qd.astype(F32) * dkg_ref[...]).astype(BF16)

    def normalise(kd_buf, vd_buf):
        for i in range(npg):
            pg = pages[i]
            r0 = i * PAGE_SIZE
            cmp_buf[r0:r0 + PAGE_SIZE, :] = pg[:, 0:2 * HEAD_DIM]
            sk = pg[:, 2 * HEAD_DIM:4 * HEAD_DIM]
            ms = _dot((sk * sk).astype(BF16), gmat_ref[0:LANES, 0:LANES])
            r = jnp.where(_lane_iota(sk.shape) < HEAD_DIM, lax.rsqrt(ms + EPS), 1.0)
            base = pl.multiple_of(c * rows_per_step + r0, PAGE_SIZE)
            stash[pl.ds(base, PAGE_SIZE), :] = (sk * r).astype(BF16)
            dk = pg[:, 4 * HEAD_DIM:4 * HEAD_DIM + DK_W]
            ms = _dot((dk * dk).astype(BF16), gmat_ref[...])
            kd_buf[r0:r0 + PAGE_SIZE, :] = (dk * lax.rsqrt(ms + EPS)).astype(BF16)
            vd_buf[r0:r0 + PAGE_SIZE, :] = pg[:, 4 * HEAD_DIM + DK_W:].astype(BF16)
        raw = _compress(cmp_buf[...] + pe_ref[...], wcol_ref[...])
        cmp_rows = rows_per_step // CMP_BLOCK
        kcvc_s[pl.ds(pl.multiple_of(c * cmp_rows, cmp_rows), cmp_rows), :] = _norm_compressed(
            raw, kcg_ref[...])

    def attend(kd_buf, vd_buf, chunk, live):
        kpos = chunk * rows_per_step + _lane_iota((1, rows_per_step))
        half = rows_per_step // 2
        sc = jnp.concatenate([_dot_nt(qd_gained, kd_buf[0:half, :]),
                              _dot_nt(qd_gained, kd_buf[half:, :])], axis=1)
        sc = sc - dslope * (past_len - kpos).astype(F32)
        m_old = dm_s[...]
        m_new = jnp.where(live, jnp.maximum(m_old, jnp.max(sc, axis=-1, keepdims=True)), m_old)
        alpha = jnp.exp(m_old - m_new)
        p = jnp.where(live, jnp.exp(sc - m_new), 0.0)
        dl_s[...] = alpha * dl_s[...] + jnp.sum(p, axis=-1, keepdims=True)
        pb = p.astype(BF16)
        dacc_s[...] = (alpha * dacc_s[...] + _dot(pb[:, 0:half], vd_buf[0:half, :])
                       + _dot(pb[:, half:], vd_buf[half:, :]))
        dm_s[...] = m_new

    @pl.when(c == 0)
    def _():
        dm_s[...] = jnp.full(dm_s.shape, NEG_INF, F32)
        dl_s[...] = jnp.zeros(dl_s.shape, F32)
        dacc_s[...] = jnp.zeros(dacc_s.shape, F32)
        kd_b[...] = jnp.zeros(kd_b.shape, BF16)
        vd_b[...] = jnp.zeros(vd_b.shape, BF16)

    @pl.when((c & 1) == 0)
    def _():
        attend(kd_b, vd_b, c - 1, c > 0)
        normalise(kd_a, vd_a)

    @pl.when((c & 1) == 1)
    def _():
        attend(kd_a, vd_a, c - 1, True)
        normalise(kd_b, vd_b)

    @pl.when(c == n_chunks - 1)
    def _():
        if (n_chunks - 1) % 2 == 0:
            attend(kd_a, vd_a, c, True)
        else:
            attend(kd_b, vd_b, c, True)
        dm, dl, dacc = dm_s[...], dl_s[...], dacc_s[...]
        kd_new = kd_new_ref[...].astype(F32)
        s_new = jnp.sum(qd.astype(F32) * kd_new, axis=-1, keepdims=True)
        m2 = jnp.maximum(dm, s_new)
        alpha = jnp.exp(dm - m2)
        p_new = jnp.exp(s_new - m2)
        l2 = alpha * dl + p_new
        acc2 = alpha * dacc + p_new * vd_new_ref[...].astype(F32)
        od = acc2 / l2
        lam = _lambda_full(lam_ref, lam_init)
        for n in range(DIFF_KV_HEADS):
            for g in range(DIFF_GROUP):
                r = (n * DIFF_GROUP + g) * 2
                lo, hi = n * DIFF_VDIM, (n + 1) * DIFF_VDIM
                d = od[r:r + 1, lo:hi] - lam * od[r + 1:r + 2, lo:hi]
                col = (n * DIFF_GROUP + g) * DIFF_VDIM
                ob_ref[:, col:col + DIFF_VDIM] = _subln(d, subln_ref[...], lam_init).astype(ob_ref.dtype)

        qn = qn_ref[...]
        aslope = _row_const(1, _alibi_slopes(nh) + [0.0] * (_QROWS - nh))
        kcvc = kcvc_s[...]
        n_cmp = kcvc.shape[0]
        cmp_end = _lane_iota((1, n_cmp)) * CMP_BLOCK + (CMP_BLOCK - 1)
        s = _dot_nt(qn, kcvc[:, :HEAD_DIM].astype(BF16)) - aslope * (past_len - cmp_end).astype(F32)
        e = jnp.exp(s - jnp.max(s, axis=-1, keepdims=True))
        p_c = e / jnp.sum(e, axis=-1, keepdims=True)
        o_c = _dot(p_c.astype(BF16), kcvc[:, HEAD_DIM:].astype(BF16))
        head_row = _row_iota((_QROWS, 1)) < nh
        p_sum = jnp.sum(jnp.where(head_row, p_c, 0.0), axis=0, keepdims=True)
        n_blk = n_cmp // 2
        imp = _pair_sum(jnp.broadcast_to(p_sum, (8, n_cmp)), LANES)[0:1]
        blk = _lane_iota((1, LANES))
        score = jnp.where(blk == 0, FORCE_SCORE, jnp.where(blk < n_blk, imp, -1.0))
        n_pick = min(TOP_N - 1, n_blk)
        sel = _rank_select(score, n_pick)
        r = _row_iota((LANES, LANES))
        upper = jnp.where(r < _lane_iota((LANES, LANES)), 1.0, 0.0).astype(BF16)
        sel8 = jnp.broadcast_to(jnp.where(sel, 1.0, 0.0), (8, LANES)).astype(BF16)
        slot_of = _dot(sel8, upper)[0:1]
        slot = _row_iota((_QROWS, 1)).astype(F32)
        block_of = jnp.sum(jnp.where(sel & (slot_of == slot), blk.astype(F32), 0.0),
                           axis=-1, keepdims=True).astype(jnp.int32)
        lane_slot = _lane_iota((1, _QROWS * SEL_BLOCK)) >> 6
        kpos = _lane_iota((1, _QROWS * SEL_BLOCK)) & (SEL_BLOCK - 1)
        for t in range(n_pick):
            b_t = block_of[t, 0]
            src = pl.multiple_of(b_t * SEL_BLOCK, SEL_BLOCK)
            sel_buf[t * SEL_BLOCK:(t + 1) * SEL_BLOCK, :] = stash[pl.ds(src, SEL_BLOCK), :]
            kpos = jnp.where(lane_slot == t, kpos + b_t * SEL_BLOCK, kpos)
        sel_buf[n_pick * SEL_BLOCK:, :] = jnp.zeros(((_QROWS - n_pick) * SEL_BLOCK, LANES), BF16)
        kv = sel_buf[...]
        qn_gained = (qn.astype(F32) * selg_ref[:, 0:HEAD_DIM]).astype(BF16)
        sc = _dot_nt(qn_gained, kv[:, :HEAD_DIM]) - aslope * (past_len - kpos).astype(F32)
        sc = jnp.where(lane_slot < n_pick, sc, NEG_INF)

        init = (jnp.full((_QROWS, 1), NEG_INF, F32), jnp.zeros((_QROWS, 1), F32),
                jnp.zeros((_QROWS, HEAD_DIM), F32))
        sm, sl_, sacc = _online_step(sc, kv[:, HEAD_DIM:], *init)
        qf = qn.astype(F32)

        def add_new(m, l, acc, kv_new):
            s_new = jnp.sum(qf * kv_new[:, :HEAD_DIM], axis=-1, keepdims=True)
            m2 = jnp.maximum(m, s_new)
            alpha = jnp.exp(m - m2)
            p_new = jnp.exp(s_new - m2)
            return (alpha * acc + p_new * kv_new[:, HEAD_DIM:]) / (alpha * l + p_new)

        o_s = add_new(sm, sl_, sacc, selkv_new_ref[...].astype(F32))

        win = win_ref[...]
        w_buf = win.shape[0]
        ms = _group_sumsq(win, gmat_ref[0:LANES, 0:LANES])
        r = jnp.where(_lane_iota(win.shape) < HEAD_DIM, lax.rsqrt(ms + EPS), 1.0)
        wkv = (win * r * wing_ref[...]).astype(BF16)
        dist_w = w_buf - _lane_iota((1, w_buf))
        sc = _dot_nt(qn, wkv[:, :HEAD_DIM]) - aslope * dist_w.astype(F32)
        sc = jnp.where(dist_w < WINDOW, sc, NEG_INF)
        o_w = add_new(*_online_step(sc, wkv[:, HEAD_DIM:], *init), winkv_new_ref[...].astype(F32))

        gate = gate_ref[...]
        for h in range(nh):
            o = (gate[:, 3 * h:3 * h + 1] * o_c[h:h + 1] + gate[:, 3 * h + 1:3 * h + 2] * o_s[h:h + 1]
                 + gate[:, 3 * h + 2:3 * h + 3] * o_w[h:h + 1])
            oa_ref[:, h * HEAD_DIM:(h + 1) * HEAD_DIM] = o.astype(oa_ref.dtype)


def _decode(l, page_table, cache, qn16, qd16, selkv_new, winkv_new, kd_new, vd_new, gates,
            win_state, p, lam_init):
    nb, n_pages = page_table.shape
    npg = _PAGES_PER_STEP
    n_chunks = n_pages // npg
    rows_per_step = npg * PAGE_SIZE
    past_len = n_pages * PAGE_SIZE

    def page_spec(i):
        return pl.BlockSpec((None, None, PAGE_SIZE, KV_W),
                            lambda b, c, pt: (l, pt[b, c * npg + i], 0, 0))

    def per_seq(a):
        if a.ndim == 4:
            return pl.BlockSpec((None, None) + a.shape[2:], lambda b, c, pt: (l, b, 0, 0))
        return pl.BlockSpec((None,) + a.shape[1:], lambda b, c, pt: (b,) + (0,) * (a.ndim - 1))

    def const(a):
        return pl.BlockSpec(a.shape, lambda b, c, pt: (0,) * a.ndim)

    seq_ins = [qn16, qd16, selkv_new, winkv_new, kd_new, vd_new, gates, win_state]
    const_ins = [p["gmat64"], p["wcol_dec"], p["pe_dec"], p["kcg"], p["selg64"], p["wing64"],
                 p["dkg64"], p["lam"], p["subln"]]
    grid_spec = pltpu.PrefetchScalarGridSpec(
        num_scalar_prefetch=1,
        grid=(nb, n_chunks),
        in_specs=[page_spec(i) for i in range(npg)] + [per_seq(a) for a in seq_ins]
        + [const(a) for a in const_ins],
        out_specs=[pl.BlockSpec((None, 1, NSA_W), lambda b, c, pt: (b, 0, 0)),
                   pl.BlockSpec((None, 1, QD_W), lambda b, c, pt: (b, 0, 0))],
        scratch_shapes=[
            pltpu.VMEM((rows_per_step, LANES), F32),
            pltpu.VMEM((rows_per_step, DK_W), BF16),
            pltpu.VMEM((rows_per_step, DV_W), BF16),
            pltpu.VMEM((rows_per_step, DK_W), BF16),
            pltpu.VMEM((rows_per_step, DV_W), BF16),
            pltpu.VMEM((past_len, LANES), BF16),
            pltpu.VMEM((_QROWS * SEL_BLOCK, LANES), BF16),
            pltpu.VMEM((past_len // CMP_BLOCK, LANES), F32),
            pltpu.VMEM((_QROWS, 1), F32),
            pltpu.VMEM((_QROWS, 1), F32),
            pltpu.VMEM((_QROWS, DV_W), F32),
        ],
    )
    return pl.pallas_call(
        functools.partial(_decode_kernel, n_chunks=n_chunks, lam_init=lam_init),
        grid_spec=grid_spec,
        out_shape=[jax.ShapeDtypeStruct((nb, 1, NSA_W), BF16),
                   jax.ShapeDtypeStruct((nb, 1, QD_W), BF16)],
        compiler_params=_cparams(2),
        name="decode",
    )(page_table, *([cache] * npg), *seq_ins, *const_ins)


def _merge_kernel(oa_ref, ob_ref, gm_ref, x_ref, ga_ref, wa_ref, wb_ref, wo_ref, o_ref):
    d = x_ref.shape[-1]
    ya = _dot(oa_ref[...], wa_ref[...])
    yb = _dot(ob_ref[...], wb_ref[...])
    mg = gm_ref[:, 0:d] * ya + gm_ref[:, d:2 * d] * yb
    o_ref[...] = x_ref[...] + ga_ref[0] * _dot(mg.astype(BF16), wo_ref[...])


def _merge(oa, ob, gm, x2d, ga, tm, p):
    n, d = x2d.shape
    steps = n // tm
    r = ga.shape[1]
    steps_per_group = steps // ga.shape[0]

    def rows(width):
        return pl.BlockSpec((tm, width), lambda i: (i, 0))

    def const(a):
        return pl.BlockSpec(a.shape, lambda i: (0,) * a.ndim)

    return pl.pallas_call(
        _merge_kernel,
        grid=(steps,),
        in_specs=[rows(NSA_W), rows(QD_W), rows(2 * d), rows(d),
                  pl.BlockSpec((1, r, d), lambda i: (i // steps_per_group, 0, 0)),
                  const(p["w_br_nsa"]), const(p["w_br_diff"]), const(p["w_out"])],
        out_specs=rows(d),
        out_shape=jax.ShapeDtypeStruct((n, d), F32),
        compiler_params=_cparams(1),
        name="merge",
    )(oa, ob, gm, x2d, ga, p["w_br_nsa"], p["w_br_diff"], p["w_out"])


def _ffn_kernel(x_ref, sh_ref, sc_ref, ga_ref, g2_ref, wi_ref, wo_ref, o_ref):
    x = x_ref[...]
    d_ff = wo_ref.shape[0]
    ms = jnp.mean(x * x, axis=-1, keepdims=True)
    h = x * lax.rsqrt(ms + EPS) * g2_ref[...]
    hb = (h * (1.0 + sc_ref[0]) + sh_ref[0]).astype(BF16)
    u = _dot(hb, wi_ref[...])
    ug = u[:, 0:d_ff]
    act = (ug * jax.nn.sigmoid(ug) * u[:, d_ff:]).astype(BF16)
    o_ref[...] = x + ga_ref[0] * _dot(act, wo_ref[...])


def _ffn(x2d, sh, sc, ga, tm, p):
    n, d = x2d.shape
    steps = n // tm
    r = sh.shape[1]
    steps_per_group = steps // sh.shape[0]
    rows = pl.BlockSpec((tm, d), lambda i: (i, 0))
    mod_spec = pl.BlockSpec((1, r, d), lambda i: (i // steps_per_group, 0, 0))

    def const(a):
        return pl.BlockSpec(a.shape, lambda i: (0,) * a.ndim, pipeline_mode=pl.Buffered(1))

    return pl.pallas_call(
        _ffn_kernel,
        grid=(steps,),
        in_specs=[rows, mod_spec, mod_spec, mod_spec, const(p["g2"]), const(p["w_ffn_in"]),
                  const(p["w_ffn_out"])],
        out_specs=rows,
        out_shape=jax.ShapeDtypeStruct((n, d), F32),
        compiler_params=_cparams(1),
        name="ffn",
    )(x2d, sh, sc, ga, p["g2"], p["w_ffn_in"], p["w_ffn_out"])


def _tiled_gain(g, reps, scale=1.0):
    return (jnp.tile(g.astype(F32), reps) * scale).reshape(1, -1)


def _slope_lanes(slopes):
    out = jnp.zeros((len(slopes), LANES), F32)
    rest = jnp.asarray(slopes, F32) * LOG2E
    for t in range(SLOPE_TERMS):
        term = rest.astype(BF16).astype(F32)
        rest = rest - term
        out = out.at[:, HEAD_DIM + 2 * t].set(term * POS_SPLIT).at[:, HEAD_DIM + 2 * t + 1].set(term)
    return out.reshape(1, -1)


def _half_gain(g):
    return jnp.concatenate([g.astype(F32), jnp.ones((HEAD_DIM,), F32)]).reshape(1, -1)


def _compress_weights(wk, wv, rows):
    col = jnp.concatenate([jnp.tile(wk.astype(F32)[:, None], (1, HEAD_DIM)),
                           jnp.tile(wv.astype(F32)[:, None], (1, HEAD_DIM))], axis=1)
    return jnp.tile(col, (rows // CMP_BLOCK, 1))


def _prepare_params(l, tm_prompt, w_ada, b_ada, norm1_g, norm2_g, w_in, nsa_q_norm, nsa_k_norm,
                    nsa_cmp_wk, nsa_cmp_wv, nsa_cmp_pe, diff_q_norm, diff_k_norm, diff_lambda,
                    diff_subln, w_br_nsa, w_br_diff, w_out, w_ffn_in, w_ffn_out):
    d = w_in.shape[1]
    wi = w_in[l]
    o_kv = NSA_W
    o_win = o_kv + KV_W
    o_gl = o_win + WIN_W
    o_qd = o_gl + GL_W
    o_mg = o_qd + QD_W
    w_r = jnp.concatenate([
        wi[:, 0:NSA_W], wi[:, o_qd:o_mg], wi[:, o_kv:o_win], wi[:, o_win:o_gl],
        wi[:, o_mg:o_mg + 2 * d], wi[:, o_gl:o_qd],
        jnp.zeros((d, LANES - GL_W), wi.dtype)], axis=1).astype(BF16)
    pe2 = jnp.concatenate([nsa_cmp_pe[l].astype(F32), jnp.zeros((CMP_BLOCK, HEAD_DIM), F32)], axis=1)
    dec_rows = _PAGES_PER_STEP * PAGE_SIZE
    nsa_slopes = _alibi_slopes(NSA_HEADS)
    diff_slopes = _alibi_slopes(DIFF_KV_HEADS * DIFF_GROUP)
    g64 = jnp.arange(2 * LANES) // HEAD_DIM
    return {
        "w_ada": w_ada[l], "b_ada": b_ada[l],
        "g1": norm1_g[l].astype(F32).reshape(1, -1), "g2": norm2_g[l].astype(F32).reshape(1, -1),
        "w_in": w_r,
        "gmat64": ((g64[:, None] == g64[None, :]) * (1.0 / HEAD_DIM)).astype(BF16),
        "qg": _tiled_gain(nsa_q_norm[l], NSA_HEADS, QK_SCALE),
        "qg_log2": _tiled_gain(nsa_q_norm[l], NSA_HEADS, QK_SCALE * LOG2E),
        "qaug": _slope_lanes(nsa_slopes),
        "dqg": _tiled_gain(diff_q_norm[l], N_QD, QK_SCALE),
        "dqg_log2": _tiled_gain(diff_q_norm[l], N_QD, QK_SCALE * LOG2E),
        "dqaug": _slope_lanes([diff_slopes[j // 2] for j in range(N_QD)]),
        "kcg": _half_gain(nsa_k_norm[l, 0]),
        "selg64": _half_gain(nsa_k_norm[l, 1]), "wing64": _half_gain(nsa_k_norm[l, 2]),
        "dkg64": _tiled_gain(diff_k_norm[l], N_KD),
        "wcol": _compress_weights(nsa_cmp_wk[l], nsa_cmp_wv[l], tm_prompt),
        "pe": jnp.tile(pe2, (tm_prompt // CMP_BLOCK, 1)),
        "wcol_dec": _compress_weights(nsa_cmp_wk[l], nsa_cmp_wv[l], dec_rows),
        "pe_dec": jnp.tile(pe2, (dec_rows // CMP_BLOCK, 1)),
        "lam": diff_lambda[l].astype(F32), "subln": diff_subln[l].astype(F32).reshape(1, -1),
        "w_br_nsa": w_br_nsa[l].astype(BF16), "w_br_diff": w_br_diff[l].astype(BF16),
        "w_out": w_out[l].astype(BF16),
        "w_ffn_in": w_ffn_in[l].astype(BF16), "w_ffn_out": w_ffn_out[l].astype(BF16),
    }


def _pad_rows(a, rows):
    return jnp.concatenate([a, jnp.zeros((a.shape[0], rows - a.shape[1]) + a.shape[2:], a.dtype)], axis=1)


def _unblock(a, n):
    return a.reshape(a.shape[0], n, LANES)[:, :, :HEAD_DIM]


def _decode_queries(qn, qd):
    nb = qn.shape[0]
    qn16 = _pad_rows(qn, _QROWS)
    q = qd.reshape(nb, DIFF_KV_HEADS, DIFF_GROUP, 2, HEAD_DIM)
    slot = jnp.arange(N_KD).reshape(DIFF_KV_HEADS, 1, 2)
    onehot = (slot[..., None] == jnp.arange(N_KD)).astype(q.dtype)
    qmat = q[:, :, :, :, None, :] * onehot[None, :, :, :, :, None]
    qd16 = _pad_rows(qmat.reshape(nb, N_QD, DK_W), _QROWS)
    return qn16, qd16


def _layer(l, xp, xs, cache_kv, state_win_kv, page_table, mod, p, tm, tiles):
    batch, seq, d = xp.shape
    nb = xs.shape[0]
    win_l = state_win_kv[l]
    lam_init = 0.8 - 0.6 * math.exp(-0.3 * l)
    mod_p = mod[:batch].reshape(batch, 1, 6 * d)
    mod_s = mod[batch:batch + nb].reshape(1, nb, 6 * d)

    def part(m, k):
        return m[:, :, k * d:(k + 1) * d]

    x2 = xp.reshape(batch * seq, d)
    (kv, win, qn, qd, ksel, kwin, vnsa, kd, vd, gl, gm, kcvc) = _inproj(
        x2, part(mod_p, 0), part(mod_p, 1), tm, p, seq, compress=True)
    oa = _nsa_prompt(qn, kcvc, ksel, kwin, vnsa, gl, batch, seq, tiles["nsa_tq"], tiles["nsa_tk"])
    ob = _diff_prompt(qd, kd, vd, p["lam"], p["subln"], batch, seq, tiles["diff_tq"],
                      tiles["diff_tk"], lam_init)
    x1 = _merge(oa, ob, gm, x2, part(mod_p, 2), tm, p)
    yp = _ffn(x1, part(mod_p, 3), part(mod_p, 4), part(mod_p, 5), tm, p)
    w_buf = win_l.shape[1]
    win_p = win.reshape(batch, seq, WIN_W)[:, seq - w_buf:]

    xs2 = xs.reshape(nb, d)
    (kv_s, win_s, qn_s, qd_s, ksel_s, kwin_s, vnsa_s, kd_s, vd_s, gl_s, gm_s) = _inproj(
        xs2, part(mod_s, 0), part(mod_s, 1), nb, p, 1, compress=False)
    qn16, qd16 = _decode_queries(_unblock(qn_s, NSA_HEADS), _unblock(qd_s, N_QD))
    selkv_s = jnp.concatenate([ksel_s[:, :HEAD_DIM], vnsa_s[:, :HEAD_DIM]], axis=1)
    winkv_s = jnp.concatenate([kwin_s[:, :HEAD_DIM], vnsa_s[:, HEAD_DIM:]], axis=1)
    kd_plain = _unblock(kd_s, N_KD).reshape(nb, DK_W)
    oa_s, ob_s = _decode(l, page_table, cache_kv, qn16, qd16, selkv_s[:, None], winkv_s[:, None],
                         kd_plain[:, None], vd_s[:, None], gl_s[:, None], state_win_kv, p, lam_init)
    x1s = _merge(oa_s.reshape(nb, NSA_W), ob_s.reshape(nb, QD_W), gm_s, xs2, part(mod_s, 2), nb, p)
    ys = _ffn(x1s, part(mod_s, 3), part(mod_s, 4), part(mod_s, 5), nb, p)
    win_state_s = jnp.concatenate([win_l[:, 1:], win_s[:, None]], axis=1)
    return (yp.reshape(batch, seq, d), ys.reshape(nb, 1, d), kv.reshape(batch, seq, KV_W),
            kv_s.reshape(nb, 1, KV_W), win_p, win_state_s)


def _tiles(seq):
    return {"nsa_tq": min(256, seq), "nsa_tk": min(1024, seq), "diff_tq": min(256, seq),
            "diff_tk": min(1024, seq)}


def kernel(x_prompt, x_sample, cache_kv, state_win_kv, page_table, c_prompt, c_sample, w_ada, b_ada,
           norm1_g, norm2_g, w_in, nsa_q_norm, nsa_k_norm, nsa_cmp_wk, nsa_cmp_wv, nsa_cmp_pe,
           diff_q_norm, diff_k_norm, diff_lambda, diff_subln, w_br_nsa, w_br_diff, w_out, w_ffn_in,
           w_ffn_out):
    depth = w_in.shape[0]
    seq = x_prompt.shape[1]
    assert x_sample.shape[1] == 1 and cache_kv.shape[2] == PAGE_SIZE
    tm = min(512, seq)
    xp, xs = x_prompt, x_sample
    kv_p, kv_s, win_p, win_s = [], [], [], []
    n_cond = c_prompt.shape[0] + c_sample.shape[0]
    c_all = jnp.concatenate([c_prompt, c_sample,
                             jnp.zeros((-n_cond % 16, c_prompt.shape[1]), c_prompt.dtype)], axis=0)
    for l in range(depth):
        p = _prepare_params(l, tm, w_ada, b_ada, norm1_g, norm2_g, w_in, nsa_q_norm, nsa_k_norm,
                            nsa_cmp_wk, nsa_cmp_wv, nsa_cmp_pe, diff_q_norm, diff_k_norm,
                            diff_lambda, diff_subln, w_br_nsa, w_br_diff, w_out, w_ffn_in, w_ffn_out)
        mod = _ada(c_all, p["w_ada"], p["b_ada"])
        xp, xs, kvp, kvs, wp, ws = _layer(l, xp, xs, cache_kv, state_win_kv, page_table, mod, p, tm,
                                          _tiles(seq))
        kv_p.append(kvp)
        kv_s.append(kvs)
        win_p.append(wp)
        win_s.append(ws)
    return (xp, xs, jnp.stack(kv_p), jnp.stack(kv_s), jnp.stack(win_p), jnp.stack(win_s))
```
